```python
import jax, jax.numpy as jnp
from jax import lax
import numpy as np

D_MODEL = 2048
BATCH = 8
SEQ = 4096
DEPTH = 1

CHUNK = 64
RET_HEADS = 4
RET_DK = 256
RET_DV = 256
ML_HEADS = 4
ML_DK = 128
ML_DV = 256
CONV_W = 4
N_EXPERTS = 64
TOP_K = 8
N_GROUPS = 8
TOPK_GROUPS = 4
D_EXPERT = 512
D_SHARED = 512
ROUTED_SCALE = 2.5
EXPERT_BLOCK = 256
ROPE_BASE = 10000.0
LN_EPS = 1e-5
GN_EPS = 1e-6
DN_ALPHA = (2 * DEPTH) ** 0.25
DN_BETA = (8 * DEPTH) ** -0.25

RET_QK_W = RET_HEADS * RET_DK
RET_V_W = RET_HEADS * RET_DV
ML_QK_W = ML_HEADS * ML_DK
ML_V_W = ML_HEADS * ML_DV
MIX_W = RET_V_W + ML_V_W
IN_W = 2 * RET_QK_W + 2 * RET_V_W + 2 * ML_QK_W + 2 * ML_V_W + 2 * ML_HEADS

kernel_name = 'hybrid_retention_mlstm_moe_deepnorm_adaln'


def layer_norm(x, g, b):
    xf = x.astype(jnp.float32)
    mu = jnp.mean(xf, axis=-1, keepdims=True)
    var = jnp.mean(jnp.square(xf - mu), axis=-1, keepdims=True)
    return ((xf - mu) * lax.rsqrt(var + LN_EPS)).astype(x.dtype) * g + b


def head_norm(h, w):
    mu = jnp.mean(h, axis=-1, keepdims=True)
    var = jnp.mean(jnp.square(h - mu), axis=-1, keepdims=True)
    return (h - mu) * lax.rsqrt(var + GN_EPS) * w.astype(jnp.float32).reshape(h.shape[-2:])


def rotary(x):
    seq, d = x.shape[1], x.shape[-1]
    inv = ROPE_BASE ** (-jnp.arange(0, d, 2, dtype=jnp.float32) / d)
    ang = jnp.arange(seq, dtype=jnp.float32)[:, None] * inv
    cos = jnp.cos(ang)[None, :, None, :]
    sin = jnp.sin(ang)[None, :, None, :]
    x1, x2 = x[..., 0::2], x[..., 1::2]
    return jnp.stack([x1 * cos - x2 * sin, x1 * sin + x2 * cos], axis=-1).reshape(x.shape)


def to_chunks(x):
    B, S, H, d = x.shape
    return x.reshape(B, S // CHUNK, CHUNK, H, d).transpose(1, 0, 3, 2, 4)


def from_chunks(x):
    NC, B, H, L, d = x.shape
    return x.transpose(1, 0, 3, 2, 4).reshape(B, NC * L, H, d)


def gate_chunks(g):
    B, S, H = g.shape
    return g.reshape(B, S // CHUNK, CHUNK, H).transpose(1, 0, 3, 2)


def causal_dwconv(x, w, b):
    y = lax.conv_general_dilated(x, w[:, None, :], window_strides=(1,),
                                 padding=[(CONV_W - 1, 0)],
                                 dimension_numbers=('NWC', 'WIO', 'NWC'),
                                 feature_group_count=x.shape[-1])
    return y + b


def retention(q, k, v):
    B, S, H, _ = q.shape
    L = CHUNK
    log_g = jnp.log1p(-jnp.exp2(-5.0 - jnp.arange(H, dtype=jnp.float32)))
    idx = jnp.arange(L, dtype=jnp.float32)
    diff = idx[:, None] - idx[None, :]
    intra_decay = jnp.where(diff >= 0, jnp.exp(log_g[:, None, None] * jnp.maximum(diff, 0.0)), 0.0)
    q_decay = jnp.exp(log_g[:, None] * (idx + 1.0))[..., None]
    k_decay = jnp.exp(log_g[:, None] * (L - 1.0 - idx))[..., None]
    chunk_decay = jnp.exp(log_g * L)[:, None, None]
    qc, kc, vc = to_chunks(q), to_chunks(k * RET_DK ** -0.5), to_chunks(v)

    def step(state, blk):
        qb, kb, vb = blk
        s = jnp.einsum('bhid,bhjd->bhij', qb, kb) * intra_decay
        out = (jnp.einsum('bhij,bhjv->bhiv', s, vb)
               + jnp.einsum('bhid,bhdv->bhiv', qb * q_decay, state))
        state = chunk_decay * state + jnp.einsum('bhjd,bhjv->bhdv', kb * k_decay, vb)
        return state, out

    state0 = jnp.zeros((B, H, RET_DK, RET_DV), jnp.float32)
    _, out = lax.scan(step, state0, (qc, kc, vc))
    return from_chunks(out)


def mlstm(q, k, v, i_pre, f_pre):
    B, S, H, _ = q.shape
    L = CHUNK
    qc, kc, vc = to_chunks(q * ML_DK ** -0.5), to_chunks(k), to_chunks(v)
    ic, lfc = gate_chunks(i_pre), gate_chunks(jax.nn.log_sigmoid(f_pre))
    causal = jnp.tril(jnp.ones((L, L), dtype=bool))

    def step(carry, blk):
        C, n, m = carry
        qb, kb, vb, ib, lfb = blk
        b = jnp.cumsum(lfb, axis=-1)
        a = b + m[..., None]
        dlog = jnp.where(causal, b[..., :, None] - b[..., None, :] + ib[..., None, :], -jnp.inf)
        m_t = jnp.maximum(a, jnp.max(dlog, axis=-1))
        w_inter = jnp.exp(a - m_t)
        w_intra = jnp.exp(dlog - m_t[..., None])
        s = jnp.einsum('bhid,bhjd->bhij', qb, kb) * w_intra
        num = (jnp.einsum('bhij,bhjv->bhiv', s, vb)
               + w_inter[..., None] * jnp.einsum('bhid,bhdv->bhiv', qb, C))
        den = jnp.sum(s, axis=-1) + w_inter * jnp.einsum('bhid,bhd->bhi', qb, n)
        h = num / jnp.maximum(jnp.abs(den), jnp.exp(-m_t))[..., None]
        m_new = m_t[..., -1]
        g_state = jnp.exp(b[..., -1] + m - m_new)
        g_k = jnp.exp(b[..., -1:] - b + ib - m_new[..., None])
        kw = kb * g_k[..., None]
        C = g_state[..., None, None] * C + jnp.einsum('bhjd,bhjv->bhdv', kw, vb)
        n = g_state[..., None] * n + jnp.sum(kw, axis=2)
        return (C, n, m_new), h

    carry0 = (jnp.zeros((B, H, ML_DK, ML_DV), jnp.float32),
              jnp.zeros((B, H, ML_DK), jnp.float32),
              jnp.zeros((B, H), jnp.float32))
    _, out = lax.scan(step, carry0, (qc, kc, vc, ic, lfc))
    return from_chunks(out)


def token_mix(u, w_in, b_gates, conv_w, conv_b, ret_norm_w, ml_norm_w, w_out):
    B, S, _ = u.shape
    f32 = jnp.float32
    proj = jnp.einsum('bsd,de->bse', u, w_in).astype(f32)
    sizes = [RET_QK_W, RET_QK_W, RET_V_W, RET_V_W, 2 * ML_QK_W, ML_V_W, ML_V_W, 2 * ML_HEADS]
    cuts = np.cumsum(sizes)[:-1].tolist()
    rq, rk, rv, rg, mqk, mv, mo, gates = jnp.split(proj, cuts, axis=-1)
    rq = rotary(rq.reshape(B, S, RET_HEADS, RET_DK))
    rk = rotary(rk.reshape(B, S, RET_HEADS, RET_DK))
    r = retention(rq, rk, rv.reshape(B, S, RET_HEADS, RET_DV))
    r = head_norm(r, ret_norm_w).reshape(B, S, RET_V_W) * jax.nn.silu(rg)
    mqk = jax.nn.silu(causal_dwconv(mqk, conv_w.astype(f32), conv_b.astype(f32)))
    mq, mk = jnp.split(mqk, 2, axis=-1)
    gates = gates + b_gates.astype(f32)
    i_pre, f_pre = gates[..., :ML_HEADS], gates[..., ML_HEADS:]
    h = mlstm(mq.reshape(B, S, ML_HEADS, ML_DK), mk.reshape(B, S, ML_HEADS, ML_DK),
              mv.reshape(B, S, ML_HEADS, ML_DV), i_pre, f_pre)
    h = head_norm(h, ml_norm_w).reshape(B, S, ML_V_W) * jax.nn.sigmoid(mo)
    mixed = jnp.concatenate([r, h], axis=-1).astype(u.dtype)
    return jnp.einsum('bsm,md->bsd', mixed, w_out)


def route(t, w_router, e_bias):
    T = t.shape[0]
    scores = jax.nn.sigmoid(jnp.einsum('td,de->te', t, w_router).astype(jnp.float32))
    biased = scores + e_bias.astype(jnp.float32)
    per_group = N_EXPERTS // N_GROUPS
    grp_score = jnp.sum(lax.top_k(biased.reshape(T, N_GROUPS, per_group), 2)[0], axis=-1)
    _, gidx = lax.top_k(grp_score, TOPK_GROUPS)
    gmask = jnp.sum(jax.nn.one_hot(gidx, N_GROUPS, dtype=jnp.float32), axis=-2) > 0
    masked = jnp.where(jnp.repeat(gmask, per_group, axis=-1), biased, -jnp.inf)
    _, eidx = lax.top_k(masked, TOP_K)
    w = jnp.take_along_axis(scores, eidx, axis=-1)
    w = w / jnp.sum(w, axis=-1, keepdims=True) * ROUTED_SCALE
    return eidx, w


def routed_experts(t, eidx, wts, w_gate, w_up, w_down):
    T, Dm = t.shape
    A = T * TOP_K
    G = EXPERT_BLOCK
    e_flat = eidx.reshape(A)
    tok_flat = jnp.repeat(jnp.arange(T, dtype=jnp.int32), TOP_K)
    order = jnp.argsort(e_flat)
    e_s, tok_s, w_s = e_flat[order], tok_flat[order], wts.reshape(A)[order]
    counts = jnp.bincount(e_flat, length=N_EXPERTS)
    start = jnp.cumsum(counts) - counts
    padded = (counts + G - 1) // G * G
    pad_end = jnp.cumsum(padded)
    pad_start = pad_end - padded
    dest = pad_start[e_s] + jnp.arange(A, dtype=pad_start.dtype) - start[e_s]
    n_blocks = -(-(A + N_EXPERTS * (G - 1)) // G)
    buf_tok = jnp.zeros((n_blocks * G,), jnp.int32).at[dest].set(tok_s)
    buf_w = jnp.zeros((n_blocks * G,), t.dtype).at[dest].set(w_s.astype(t.dtype))
    block_e = jnp.minimum(jnp.searchsorted(pad_end, jnp.arange(n_blocks) * G, side='right'),
                          N_EXPERTS - 1)

    def body(acc, blk):
        tok_b, w_b, e_b = blk
        xb = t[tok_b]
        hb = jax.nn.silu(xb @ w_gate[e_b]) * (xb @ w_up[e_b])
        return acc.at[tok_b].add((hb @ w_down[e_b]) * w_b[:, None]), None

    acc, _ = lax.scan(body, jnp.zeros_like(t),
                      (buf_tok.reshape(n_blocks, G), buf_w.reshape(n_blocks, G), block_e))
    return acc


def channel_mix(u, w_router, e_bias, w_gate, w_up, w_down, ws_gate, ws_up, ws_down):
    B, S, Dm = u.shape
    t = u.reshape(B * S, Dm)
    eidx, wts = route(t, w_router, e_bias)
    routed = routed_experts(t, eidx, wts, w_gate, w_up, w_down)
    shared = (jax.nn.silu(t @ ws_gate) * (t @ ws_up)) @ ws_down
    return (routed + shared).reshape(B, S, Dm)


def setup_inputs(seed: int = 0) -> dict:
    key = jax.random.key(seed)
    ks = jax.random.split(key, 24)
    nrm = jax.random.normal
    f32 = jnp.float32
    Dm, E, F, Fs = D_MODEL, N_EXPERTS, D_EXPERT, D_SHARED
    f_bias = jnp.linspace(3.0, 6.0, ML_HEADS, dtype=f32)
    b_gates = jnp.concatenate([0.1 * nrm(ks[4], (DEPTH, ML_HEADS), f32),
                               f_bias + 0.1 * nrm(ks[5], (DEPTH, ML_HEADS), f32)], axis=-1)
    return {
        'x': nrm(ks[0], (BATCH, SEQ, Dm), f32),
        'c': nrm(ks[1], (BATCH, Dm), f32),
        'w_ada': nrm(ks[2], (DEPTH, Dm, 6 * Dm), f32) * Dm ** -0.5,
        'b_ada': 0.02 * nrm(ks[3], (DEPTH, 6 * Dm), f32),
        'w_in': nrm(ks[6], (DEPTH, Dm, IN_W), f32) * Dm ** -0.5,
        'b_gates': b_gates,
        'conv_w': nrm(ks[7], (DEPTH, CONV_W, 2 * ML_QK_W), f32) * CONV_W ** -0.5,
        'conv_b': 0.02 * nrm(ks[8], (DEPTH, 2 * ML_QK_W), f32),
        'ret_norm_w': 1.0 + 0.02 * nrm(ks[9], (DEPTH, RET_V_W), f32),
        'ml_norm_w': 1.0 + 0.02 * nrm(ks[10], (DEPTH, ML_V_W), f32),
        'w_out': nrm(ks[11], (DEPTH, MIX_W, Dm), f32) * MIX_W ** -0.5 * DN_BETA,
        'ln1_g': 1.0 + 0.02 * nrm(ks[12], (DEPTH, Dm), f32),
        'ln1_b': 0.02 * nrm(ks[13], (DEPTH, Dm), f32),
        'w_router': nrm(ks[14], (DEPTH, Dm, E), f32) * Dm ** -0.5,
        'e_bias': 0.01 * nrm(ks[15], (DEPTH, E), f32),
        'w_gate': nrm(ks[16], (DEPTH, E, Dm, F), f32) * Dm ** -0.5,
        'w_up': nrm(ks[17], (DEPTH, E, Dm, F), f32) * Dm ** -0.5,
        'w_down': nrm(ks[18], (DEPTH, E, F, Dm), f32) * F ** -0.5 * DN_BETA,
        'ws_gate': nrm(ks[19], (DEPTH, Dm, Fs), f32) * Dm ** -0.5,
        'ws_up': nrm(ks[20], (DEPTH, Dm, Fs), f32) * Dm ** -0.5,
        'ws_down': nrm(ks[21], (DEPTH, Fs, Dm), f32) * Fs ** -0.5 * DN_BETA,
        'ln2_g': 1.0 + 0.02 * nrm(ks[22], (DEPTH, Dm), f32),
        'ln2_b': 0.02 * nrm(ks[23], (DEPTH, Dm), f32),
    }


def reference(x, c, w_ada, b_ada, w_in, b_gates, conv_w, conv_b, ret_norm_w, ml_norm_w, w_out,
              ln1_g, ln1_b, w_router, e_bias, w_gate, w_up, w_down, ws_gate, ws_up, ws_down,
              ln2_g, ln2_b):
    for l in range(DEPTH):
        mod = jnp.einsum('bd,de->be', jax.nn.silu(c), w_ada[l]) + b_ada[l]
        sh1, sc1, g1, sh2, sc2, g2 = jnp.split(mod[:, None, :], 6, axis=-1)
        u = x * (1 + sc1) + sh1
        mix = token_mix(u, w_in[l], b_gates[l], conv_w[l], conv_b[l], ret_norm_w[l],
                        ml_norm_w[l], w_out[l])
        x = layer_norm(DN_ALPHA * x + g1 * mix, ln1_g[l], ln1_b[l])
        u = x * (1 + sc2) + sh2
        y = channel_mix(u, w_router[l], e_bias[l], w_gate[l], w_up[l], w_down[l],
                        ws_gate[l], ws_up[l], ws_down[l])
        x = layer_norm(DN_ALPHA * x + g2 * y, ln2_g[l], ln2_b[l])
    return x
```

```python
import functools
import math

import numpy as np
import jax
import jax.numpy as jnp
from jax import lax
from jax.experimental import pallas as pl
from jax.experimental.pallas import tpu as pltpu

F32 = jnp.float32
BF16 = jnp.bfloat16
U32 = jnp.uint32
I32 = jnp.int32

DEPTH = 1
CHUNK = 64
RET_HEADS = 4
RET_DK = 256
RET_DV = 256
ML_HEADS = 4
ML_DK = 128
ML_DV = 256
CONV_W = 4
N_EXPERTS = 64
TOP_K = 8
N_GROUPS = 8
TOPK_GROUPS = 4
GROUP_SIZE = N_EXPERTS // N_GROUPS
ROUTED_SCALE = 2.5
ROPE_BASE = 10000.0
LN_EPS = 1e-5
GN_EPS = 1e-6
DN_ALPHA = (2 * DEPTH) ** 0.25

RET_QK_W = RET_HEADS * RET_DK
RET_V_W = RET_HEADS * RET_DV
ML_QK_W = ML_HEADS * ML_DK
ML_V_W = ML_HEADS * ML_DV

LANES = 128
SUBLANES = 8
MXU_N = 256
VMEM_LIMIT = 56 * 1024 * 1024

EXPERT_ROWS = 512
GATHER_ROWS = 512


def _cparams(sem):
    return pltpu.CompilerParams(dimension_semantics=sem, vmem_limit_bytes=VMEM_LIMIT)


def _dot(a, b):
    return jnp.dot(a, b, preferred_element_type=F32)


def _dot_nt(a, b):
    return lax.dot_general(a, b, (((1,), (1,)), ((), ())), preferred_element_type=F32)


def _dot_tn(a, b):
    return lax.dot_general(a, b, (((0,), (0,)), ((), ())), preferred_element_type=F32)


def _sigmoid(x):
    return 1.0 / (1.0 + jnp.exp(-x))


def _silu(x):
    return x * _sigmoid(x)


def _pack_rows(ref, val):
    n, d = val.shape
    half = d // 2
    bits = lax.bitcast_convert_type(val.astype(BF16).astype(F32), U32)
    for s in range(SUBLANES):
        lo = bits[:, s * LANES:(s + 1) * LANES] >> 16
        hi = bits[:, half + s * LANES: half + (s + 1) * LANES] & jnp.uint32(0xFFFF0000)
        ref[pl.ds(s, n, stride=SUBLANES), :] = lo | hi


def _unpack_words(w):
    lo = lax.bitcast_convert_type(w << 16, F32)
    hi = lax.bitcast_convert_type(w & jnp.uint32(0xFFFF0000), F32)
    return lo, hi


def _unpack_rows(ref, n):
    los, his = [], []
    for s in range(SUBLANES):
        lo, hi = _unpack_words(ref[pl.ds(s, n, stride=SUBLANES), :])
        los.append(lo)
        his.append(hi)
    return jnp.concatenate(los + his, axis=1)


def _ada_kernel(c_ref, w_ref, b_ref, o_ref):
    s = _silu(c_ref[...]).astype(BF16)
    o_ref[...] = _dot(s, w_ref[...].astype(BF16)) + b_ref[...]


def _ada_mod(c, w_ada, b_ada):
    bsz, d = c.shape
    n = w_ada.shape[1]
    tn = 512
    return pl.pallas_call(
        _ada_kernel,
        grid=(n // tn,),
        in_specs=[pl.BlockSpec((bsz, d), lambda j: (0, 0)),
                  pl.BlockSpec((d, tn), lambda j: (0, j)),
                  pl.BlockSpec((1, tn), lambda j: (0, j))],
        out_specs=pl.BlockSpec((bsz, tn), lambda j: (0, j)),
        out_shape=jax.ShapeDtypeStruct((bsz, n), F32),
        compiler_params=_cparams(("arbitrary",)),
        name="ada_mod",
    )(c, w_ada, b_ada.reshape(1, n))


def _inproj_kernel(x_ref, sc_ref, sh_ref, w_ref, cos_ref, sin_ref, pb_ref, pf_ref, u_scr,
                   *, n_rot, n_b):
    j = pl.program_id(1)

    @pl.when(j == 0)
    def _():
        u_scr[...] = (x_ref[...] * (1.0 + sc_ref[...]) + sh_ref[...]).astype(BF16)

    acc = _dot(u_scr[...], w_ref[...])

    @pl.when(j < n_rot)
    def _():
        half = acc.shape[1] // 2
        x1 = acc[:, :half]
        x2 = acc[:, half:]
        c = cos_ref[...]
        s = sin_ref[...]
        scale = jnp.where(j < n_rot // 2, 1.0, RET_DK ** -0.5).astype(F32)
        o = jnp.concatenate([x1 * c - x2 * s, x1 * s + x2 * c], axis=1) * scale
        pb_ref[...] = o.astype(BF16)

    @pl.when((j >= n_rot) & (j < n_b))
    def _():
        pb_ref[...] = acc.astype(BF16)

    @pl.when(j >= n_b)
    def _():
        pf_ref[...] = acc


def _in_proj(x2d, mod3, w_all, cos_t, sin_t, seq, tm):
    t, d = x2d.shape
    n_all = w_all.shape[1]
    tn = MXU_N
    n_tiles = n_all // tn
    n_rot = 2 * RET_QK_W // tn
    n_b = (2 * RET_QK_W + RET_V_W + ML_V_W) // tn
    n_f = n_tiles - n_b
    tiles_per_seq = seq // tm
    kern = functools.partial(_inproj_kernel, n_rot=n_rot, n_b=n_b)
    return pl.pallas_call(
        kern,
        grid=(t // tm, n_tiles),
        in_specs=[
            pl.BlockSpec((tm, d), lambda i, j: (i, 0)),
            pl.BlockSpec((None, 1, d), lambda i, j: ((i // tiles_per_seq) * 6 + 1, 0, 0)),
            pl.BlockSpec((None, 1, d), lambda i, j: ((i // tiles_per_seq) * 6 + 0, 0, 0)),
            pl.BlockSpec((d, tn), lambda i, j: (0, j)),
            pl.BlockSpec((tm, tn // 2), lambda i, j: (i % tiles_per_seq, 0)),
            pl.BlockSpec((tm, tn // 2), lambda i, j: (i % tiles_per_seq, 0)),
        ],
        out_specs=[
            pl.BlockSpec((tm, tn), lambda i, j: (i, jnp.minimum(j, n_b - 1))),
            pl.BlockSpec((tm, tn), lambda i, j: (i, jnp.maximum(j - n_b, 0))),
        ],
        out_shape=[jax.ShapeDtypeStruct((t, n_b * tn), BF16),
                   jax.ShapeDtypeStruct((t, n_f * tn), F32)],
        scratch_shapes=[pltpu.VMEM((tm, d), BF16)],
        compiler_params=_cparams(("arbitrary", "arbitrary")),
        name="in_proj",
    )(x2d, mod3, mod3, w_all, cos_t, sin_t)


def _head_norm(h, w):
    mu = jnp.mean(h, axis=-1, keepdims=True)
    hc = h - mu
    var = jnp.mean(hc * hc, axis=-1, keepdims=True)
    return hc * lax.rsqrt(var + GN_EPS) * w


def _retention_kernel(q_ref, k_ref, v_ref, g_ref, nw_ref, o_ref, state_ref, *, n_chunks):
    L = CHUNK

    @pl.when(pl.program_id(1) == 0)
    def _():
        state_ref[...] = jnp.zeros_like(state_ref)

    ii = lax.broadcasted_iota(I32, (L, L), 0)
    jj = lax.broadcasted_iota(I32, (L, L), 1)
    diff = (ii - jj).astype(F32)
    causal = ii >= jj
    row = lax.broadcasted_iota(I32, (L, 1), 0).astype(F32)
    consts = []
    for h in range(RET_HEADS):
        log_g = math.log1p(-(2.0 ** (-5.0 - h)))
        intra = jnp.where(causal, jnp.exp(log_g * jnp.maximum(diff, 0.0)), 0.0)
        q_decay = jnp.exp(log_g * (row + 1.0))
        k_decay = jnp.exp(log_g * (L - 1.0 - row))
        consts.append((intra, q_decay, k_decay, math.exp(log_g * L)))

    def chunk_body(c, carry):
        rows = pl.ds(pl.multiple_of(c * L, L), L)
        for h in range(RET_HEADS):
            intra, q_decay, k_decay, chunk_decay = consts[h]
            qs = slice(h * RET_DK, (h + 1) * RET_DK)
            vs = slice(h * RET_DV, (h + 1) * RET_DV)
            qb = q_ref[rows, qs]
            kb = k_ref[rows, qs]
            vb = v_ref[rows, vs]
            st = state_ref[h]
            s = _dot_nt(qb, kb) * intra
            out = _dot(s.astype(BF16), vb) + q_decay * _dot(qb, st.astype(BF16))
            kd = (kb.astype(F32) * k_decay).astype(BF16)
            state_ref[h] = chunk_decay * st + _dot_tn(kd, vb)
            r = _head_norm(out, nw_ref[:, vs]) * _silu(g_ref[rows, vs])
            o_ref[rows, vs] = r.astype(BF16)
        return carry

    lax.fori_loop(0, n_chunks, chunk_body, 0)


def _retention(pb, pf, norm_w, bsz, seq, ts):
    t = pb.shape[0]
    n_s = seq // ts
    kern = functools.partial(_retention_kernel, n_chunks=ts // CHUNK)
    blk = lambda col: pl.BlockSpec((ts, RET_QK_W), lambda b, s: (b * n_s + s, col))
    return pl.pallas_call(
        kern,
        grid=(bsz, n_s),
        in_specs=[blk(0), blk(1), blk(2), blk(0),
                  pl.BlockSpec((1, RET_V_W), lambda b, s: (0, 0))],
        out_specs=pl.BlockSpec((ts, RET_V_W), lambda b, s: (b * n_s + s, 0)),
        out_shape=jax.ShapeDtypeStruct((t, RET_V_W), BF16),
        scratch_shapes=[pltpu.VMEM((RET_HEADS, RET_DK, RET_DV), F32)],
        compiler_params=_cparams(("arbitrary", "arbitrary")),
        name="retention",
    )(pb, pb, pb, pf, norm_w.reshape(1, RET_V_W))


def _split_hi_lo(x):
    hi = x.astype(BF16)
    lo = (x - hi.astype(F32)).astype(BF16)
    return hi, lo


def _mlstm_kernel(qk_ref, v_ref, o_ref, gt_ref, bg_ref, cw_ref, cb_ref, nw_ref, out_ref,
                  ext_ref, act_ref, c_ref, n_ref, m_ref, *, ts):
    L = CHUNK
    n_chunks = ts // L
    pad = SUBLANES

    @pl.when(pl.program_id(1) == 0)
    def _():
        ext_ref[0:pad, :] = jnp.zeros((pad, 2 * ML_QK_W), F32)
        c_ref[...] = jnp.zeros_like(c_ref)
        n_ref[...] = jnp.zeros_like(n_ref)
        m_ref[...] = jnp.zeros_like(m_ref)

    ext_ref[pad:pad + ts, :] = qk_ref[...]
    conv = cb_ref[...] + jnp.zeros((ts, 2 * ML_QK_W), F32)
    for w in range(CONV_W):
        off = pad - (CONV_W - 1) + w
        conv = conv + ext_ref[off:off + ts, :] * cw_ref[w:w + 1, :]
    act_ref[...] = _silu(conv)
    ext_ref[0:pad, :] = qk_ref[ts - pad:ts, :]

    g = gt_ref[...] + bg_ref[...]
    lf = jnp.minimum(g, 0.0) - jnp.log(1.0 + jnp.exp(-jnp.abs(g)))
    ti = lax.broadcasted_iota(I32, (ts, ts), 0)
    tj = lax.broadcasted_iota(I32, (ts, ts), 1)
    shift = L.bit_length() - 1
    tri = jnp.where((ti >= tj) & ((ti >> shift) == (tj >> shift)), 1.0, 0.0).astype(BF16)
    lf_hi, lf_lo = _split_hi_lo(lf)
    bcum = _dot(tri, lf_hi) + _dot(tri, lf_lo)
    rvec = g - pltpu.roll(bcum, LANES - ML_HEADS, 1)

    ii = lax.broadcasted_iota(I32, (L, L), 0)
    jj = lax.broadcasted_iota(I32, (L, L), 1)
    causal = ii >= jj
    lane = lax.broadcasted_iota(I32, (L, LANES), 1)

    for c in range(n_chunks):
        rows = slice(c * L, (c + 1) * L)
        r_hi, r_lo = _split_hi_lo(rvec[rows, :])
        for h in range(ML_HEADS):
            sel = jnp.where(lane == h, 1.0, 0.0).astype(BF16)
            rmat = _dot_nt(sel, r_hi) + _dot_nt(sel, r_lo)
            r_col = rvec[rows, h:h + 1]
            b_col = bcum[rows, ML_HEADS + h:ML_HEADS + h + 1]
            m_old = m_ref[h:h + 1, 0:1]
            cmax = jnp.max(jnp.where(causal, rmat, -jnp.inf), axis=1, keepdims=True)
            big_m = jnp.maximum(m_old, cmax)
            dmat = jnp.where(causal, jnp.exp(rmat - big_m), 0.0)
            w_inter = jnp.exp(m_old - big_m)
            floor = jnp.exp(-(b_col + big_m))

            qs = slice(h * ML_DK, (h + 1) * ML_DK)
            ks = slice(ML_QK_W + h * ML_DK, ML_QK_W + (h + 1) * ML_DK)
            vs = slice(h * ML_DV, (h + 1) * ML_DV)
            qf = act_ref[rows, qs] * (ML_DK ** -0.5)
            kf = act_ref[rows, ks]
            qb = qf.astype(BF16)
            vb = v_ref[rows, vs]
            cst = c_ref[h]
            nst = n_ref[h:h + 1, :]

            s = _dot_nt(qb, kf.astype(BF16)) * dmat
            num = _dot(s.astype(BF16), vb) + w_inter * _dot(qb, cst.astype(BF16))
            qn = jnp.sum(qb.astype(F32) * nst, axis=1, keepdims=True)
            den = jnp.sum(s, axis=1, keepdims=True) + w_inter * qn
            hout = num * (1.0 / jnp.maximum(jnp.abs(den), floor))

            m_last = big_m[L - 1:L, :]
            m_new = b_col[L - 1:L, :] + m_last
            g_state = jnp.exp(m_old - m_last)
            g_k = jnp.exp(r_col - m_last)
            kw = kf * g_k
            c_ref[h] = g_state * cst + _dot_tn(kw.astype(BF16), vb)
            n_ref[h:h + 1, :] = g_state * nst + jnp.sum(kw, axis=0, keepdims=True)
            m_ref[h:h + 1, :] = jnp.broadcast_to(m_new, (1, LANES))

            y = _head_norm(hout, nw_ref[:, vs]) * _sigmoid(o_ref[rows, vs])
            out_ref[rows, vs] = y.astype(BF16)


def _mlstm(pb, pf, b_gates, conv_w, conv_b, norm_w, bsz, seq, ts):
    t = pb.shape[0]
    n_s = seq // ts
    bg = jnp.zeros((1, LANES), F32).at[0, :2 * ML_HEADS].set(b_gates)
    kern = functools.partial(_mlstm_kernel, ts=ts)
    wide = 2 * ML_QK_W
    gate_blk = (RET_V_W + wide + ML_V_W) // LANES
    return pl.pallas_call(
        kern,
        grid=(bsz, n_s),
        in_specs=[
            pl.BlockSpec((ts, wide), lambda b, s: (b * n_s + s, RET_V_W // wide)),
            pl.BlockSpec((ts, ML_V_W), lambda b, s: (b * n_s + s, 3)),
            pl.BlockSpec((ts, ML_V_W), lambda b, s: (b * n_s + s, (RET_V_W + wide) // ML_V_W)),
            pl.BlockSpec((ts, LANES), lambda b, s: (b * n_s + s, gate_blk)),
            pl.BlockSpec((1, LANES), lambda b, s: (0, 0)),
            pl.BlockSpec((CONV_W, wide), lambda b, s: (0, 0)),
            pl.BlockSpec((1, wide), lambda b, s: (0, 0)),
            pl.BlockSpec((1, ML_V_W), lambda b, s: (0, 0)),
        ],
        out_specs=pl.BlockSpec((ts, ML_V_W), lambda b, s: (b * n_s + s, 0)),
        out_shape=jax.ShapeDtypeStruct((t, ML_V_W), BF16),
        scratch_shapes=[
            pltpu.VMEM((ts + SUBLANES, wide), F32),
            pltpu.VMEM((ts, wide), F32),
            pltpu.VMEM((ML_HEADS, ML_DK, ML_DV), F32),
            pltpu.VMEM((SUBLANES, ML_DK), F32),
            pltpu.VMEM((SUBLANES, LANES), F32),
        ],
        compiler_params=_cparams(("arbitrary", "arbitrary")),
        name="mlstm",
    )(pf, pb, pf, pf, bg, conv_w, conv_b.reshape(1, wide), norm_w.reshape(1, ML_V_W))


def _layer_norm(z, g, b):
    mu = jnp.mean(z, axis=-1, keepdims=True)
    zc = z - mu
    var = jnp.mean(zc * zc, axis=-1, keepdims=True)
    return zc * lax.rsqrt(var + LN_EPS) * g + b


def _outproj_kernel(r_ref, h_ref, w1_ref, w2_ref, x_ref, g1_ref, sc_ref, sh_ref, lng_ref, lnb_ref,
                    x1_ref, u2_ref):
    mix = _dot(r_ref[...], w1_ref[...]) + _dot(h_ref[...], w2_ref[...])
    z = DN_ALPHA * x_ref[...] + g1_ref[...] * mix
    x1 = _layer_norm(z, lng_ref[...], lnb_ref[...])
    x1_ref[...] = x1
    _pack_rows(u2_ref, x1 * (1.0 + sc_ref[...]) + sh_ref[...])


def _out_proj(r_mix, h_mix, w_out_bf, x2d, mod3, ln_g, ln_b, seq, tm):
    t, d = x2d.shape
    tiles_per_seq = seq // tm
    modblk = lambda k: pl.BlockSpec((None, 1, d), lambda i: ((i // tiles_per_seq) * 6 + k, 0, 0))
    return pl.pallas_call(
        _outproj_kernel,
        grid=(t // tm,),
        in_specs=[
            pl.BlockSpec((tm, RET_V_W), lambda i: (i, 0)),
            pl.BlockSpec((tm, ML_V_W), lambda i: (i, 0)),
            pl.BlockSpec((RET_V_W, d), lambda i: (0, 0)),
            pl.BlockSpec((ML_V_W, d), lambda i: (1, 0)),
            pl.BlockSpec((tm, d), lambda i: (i, 0)),
            modblk(2), modblk(4), modblk(3),
            pl.BlockSpec((1, d), lambda i: (0, 0)),
            pl.BlockSpec((1, d), lambda i: (0, 0)),
        ],
        out_specs=[pl.BlockSpec((tm, d), lambda i: (i, 0)),
                   pl.BlockSpec((tm * SUBLANES, LANES), lambda i: (i, 0))],
        out_shape=[jax.ShapeDtypeStruct((t, d), F32),
                   jax.ShapeDtypeStruct((t * SUBLANES, LANES), U32)],
        compiler_params=_cparams(("arbitrary",)),
        name="out_proj",
    )(r_mix, h_mix, w_out_bf, w_out_bf, x2d, mod3, mod3, mod3,
      ln_g.reshape(1, d), ln_b.reshape(1, d))


def _router_kernel(u_ref, w_ref, eb_ref, eidx_ref, wts_ref, rank_ref, cnt_ref, carry_ref, *, tm):
    @pl.when(pl.program_id(0) == 0)
    def _():
        carry_ref[...] = jnp.zeros_like(carry_ref)

    x = _unpack_rows(u_ref, tm).astype(BF16)
    logits = _dot_nt(w_ref[...], x)
    scores = _sigmoid(logits)
    biased = scores + eb_ref[:, 0:1]
    neg = -jnp.inf
    gi = lax.broadcasted_iota(I32, (N_GROUPS, tm), 0)
    sc = [scores[j * N_GROUPS:(j + 1) * N_GROUPS, :] for j in range(GROUP_SIZE)]
    bs = [biased[j * N_GROUPS:(j + 1) * N_GROUPS, :] for j in range(GROUP_SIZE)]
    eid = [gi * GROUP_SIZE + j for j in range(GROUP_SIZE)]

    m1 = functools.reduce(jnp.maximum, bs)
    first = functools.reduce(jnp.minimum,
                             [jnp.where(bs[j] == m1, j, GROUP_SIZE) for j in range(GROUP_SIZE)])
    m2 = functools.reduce(jnp.maximum,
                          [jnp.where(first == j, neg, bs[j]) for j in range(GROUP_SIZE)])
    grp = m1 + m2

    keep = jnp.zeros((N_GROUPS, tm), dtype=jnp.bool_)
    for _ in range(TOPK_GROUPS):
        top = jnp.max(grp, axis=0, keepdims=True)
        pick = jnp.min(jnp.where(grp == top, gi, N_GROUPS), axis=0, keepdims=True)
        hit = gi == pick
        keep = keep | hit
        grp = jnp.where(hit, neg, grp)

    cand = [jnp.where(keep, bs[j], neg) for j in range(GROUP_SIZE)]
    chosen = [jnp.zeros((N_GROUPS, tm), dtype=jnp.bool_) for _ in range(GROUP_SIZE)]
    picks, vals = [], []
    for _ in range(TOP_K):
        top = jnp.max(functools.reduce(jnp.maximum, cand), axis=0, keepdims=True)
        e_pick = jnp.min(functools.reduce(
            jnp.minimum, [jnp.where(cand[j] == top, eid[j], N_EXPERTS) for j in range(GROUP_SIZE)]),
            axis=0, keepdims=True)
        hits = [eid[j] == e_pick for j in range(GROUP_SIZE)]
        val = jnp.sum(functools.reduce(
            jnp.add, [jnp.where(hits[j], sc[j], 0.0) for j in range(GROUP_SIZE)]),
            axis=0, keepdims=True)
        cand = [jnp.where(hits[j], neg, cand[j]) for j in range(GROUP_SIZE)]
        chosen = [chosen[j] | hits[j] for j in range(GROUP_SIZE)]
        picks.append(e_pick)
        vals.append(val)

    total = functools.reduce(jnp.add, vals)
    inv = ROUTED_SCALE / total
    eidx_ref[...] = jnp.concatenate(picks, axis=0)
    wts_ref[...] = jnp.concatenate([v * inv for v in vals], axis=0)

    onehot = jnp.concatenate([jnp.where(chosen[j], 1.0, 0.0) for j in range(GROUP_SIZE)], axis=0)
    ti = lax.broadcasted_iota(I32, (tm, tm), 0)
    tj = lax.broadcasted_iota(I32, (tm, tm), 1)
    before = jnp.where(ti < tj, 1.0, 0.0).astype(BF16)
    carry = carry_ref[...]
    prior = _dot(onehot.astype(BF16), before) + carry[:, 0:1]
    pr = [prior[j * N_GROUPS:(j + 1) * N_GROUPS, :] for j in range(GROUP_SIZE)]
    ranks = []
    for k in range(TOP_K):
        ranks.append(jnp.sum(functools.reduce(
            jnp.add, [jnp.where(eid[j] == picks[k], pr[j], 0.0) for j in range(GROUP_SIZE)]),
            axis=0, keepdims=True))
    rank_ref[...] = jnp.concatenate(ranks, axis=0).astype(I32)
    carry = carry + jnp.sum(onehot, axis=1, keepdims=True)
    carry_ref[...] = carry
    cnt_ref[...] = carry.astype(I32)


def _router(u2p, w_router, e_bias, t, tm):
    d = w_router.shape[0]
    perm = np.arange(N_EXPERTS).reshape(N_GROUPS, GROUP_SIZE).T.reshape(-1)
    w_t = w_router.T[perm].astype(BF16)
    eb = jnp.broadcast_to(e_bias[perm][:, None], (N_EXPERTS, LANES)).astype(F32)
    kern = functools.partial(_router_kernel, tm=tm)
    eidx, wts, rank, cnt = pl.pallas_call(
        kern,
        grid=(t // tm,),
        in_specs=[pl.BlockSpec((tm * SUBLANES, LANES), lambda i: (i, 0)),
                  pl.BlockSpec((N_EXPERTS, d), lambda i: (0, 0)),
                  pl.BlockSpec((N_EXPERTS, LANES), lambda i: (0, 0))],
        out_specs=[pl.BlockSpec((TOP_K, tm), lambda i: (0, i)),
                   pl.BlockSpec((TOP_K, tm), lambda i: (0, i)),
                   pl.BlockSpec((TOP_K, tm), lambda i: (0, i)),
                   pl.BlockSpec((N_EXPERTS, LANES), lambda i: (0, 0))],
        out_shape=[jax.ShapeDtypeStruct((TOP_K, t), I32),
                   jax.ShapeDtypeStruct((TOP_K, t), F32),
                   jax.ShapeDtypeStruct((TOP_K, t), I32),
                   jax.ShapeDtypeStruct((N_EXPERTS, LANES), I32)],
        scratch_shapes=[pltpu.VMEM((N_EXPERTS, LANES), F32)],
        compiler_params=_cparams(("arbitrary",)),
        name="router",
    )(u2p, w_t, eb)
    counts = jnp.zeros((N_EXPERTS,), I32).at[perm].set(cnt[:, 0])
    return eidx, wts, rank, counts


def _gather_kernel(nsteps_ref, idx_ref, src_ref, dst_ref, sem, *, rows):
    i = pl.program_id(0)

    def row_copy(src_row, dst_row):
        return pltpu.make_async_copy(
            src_ref.at[pl.ds(pl.multiple_of(src_row * SUBLANES, SUBLANES), SUBLANES)],
            dst_ref.at[pl.ds(pl.multiple_of(dst_row * SUBLANES, SUBLANES), SUBLANES)],
            sem)

    @pl.when(i < nsteps_ref[0])
    def _():
        base = i * rows

        def issue(r, carry):
            row_copy(idx_ref[0, 0, r], base + r).start()
            return carry

        lax.fori_loop(0, rows, issue, 0)

        def drain(r, carry):
            row_copy(0, base + r).wait()
            return carry

        lax.fori_loop(0, rows, drain, 0)


def _row_gather(src, idx, n_steps_used, rows):
    n = idx.shape[0]
    steps = n // rows
    kern = functools.partial(_gather_kernel, rows=rows)
    grid_spec = pltpu.PrefetchScalarGridSpec(
        num_scalar_prefetch=1,
        grid=(steps,),
        in_specs=[pl.BlockSpec((1, 1, rows), lambda i, ns: (i, 0, 0), memory_space=pltpu.SMEM),
                  pl.BlockSpec(memory_space=pl.ANY)],
        out_specs=pl.BlockSpec(memory_space=pl.ANY),
        scratch_shapes=[pltpu.SemaphoreType.DMA],
    )
    return pl.pallas_call(
        kern,
        grid_spec=grid_spec,
        out_shape=jax.ShapeDtypeStruct((n * SUBLANES, LANES), U32),
        compiler_params=_cparams(("arbitrary",)),
        name="row_gather",
    )(n_steps_used, idx.reshape(steps, 1, rows), src)


def _expert_kernel(be_ref, nu_ref, xs_ref, wg_ref, wu_ref, wd_ref, ys_ref, *, rows):
    @pl.when(pl.program_id(0) < nu_ref[0])
    def _():
        x = _unpack_rows(xs_ref, rows).astype(BF16)
        gate = _dot(x, wg_ref[...])
        up = _dot(x, wu_ref[...])
        hidden = (_silu(gate) * up).astype(BF16)
        _pack_rows(ys_ref, _dot(hidden, wd_ref[...]))


def _experts(xs, block_e, n_used, wg, wu, wd, rows):
    n_blocks = block_e.shape[0]
    _, d, f = wg.shape
    kern = functools.partial(_expert_kernel, rows=rows)
    row_blk = pl.BlockSpec((rows * SUBLANES, LANES),
                           lambda i, be, nu: (jnp.minimum(i, nu[0] - 1), 0))
    grid_spec = pltpu.PrefetchScalarGridSpec(
        num_scalar_prefetch=2,
        grid=(n_blocks,),
        in_specs=[row_blk,
                  pl.BlockSpec((None, d, f), lambda i, be, nu: (be[i], 0, 0)),
                  pl.BlockSpec((None, d, f), lambda i, be, nu: (be[i], 0, 0)),
                  pl.BlockSpec((None, f, d), lambda i, be, nu: (be[i], 0, 0))],
        out_specs=row_blk,
    )
    return pl.pallas_call(
        kern,
        grid_spec=grid_spec,
        out_shape=jax.ShapeDtypeStruct(xs.shape, U32),
        compiler_params=_cparams(("arbitrary",)),
        name="experts",
    )(block_e, n_used, xs, wg, wu, wd)


def _combine_kernel(yg_ref, w_ref, u_ref, x1_ref, g2_ref, wsg_ref, wsu_ref, wsd_ref,
                    lng_ref, lnb_ref, o_ref, y_scr, *, tm):
    half = (SUBLANES * LANES)
    wk = [jnp.broadcast_to(w_ref[:, k:k + 1], (tm, LANES)) for k in range(TOP_K)]
    for s in range(SUBLANES):
        lo_acc = jnp.zeros((tm, LANES), F32)
        hi_acc = jnp.zeros((tm, LANES), F32)
        for k in range(TOP_K):
            lo, hi = _unpack_words(yg_ref[pl.ds(k * SUBLANES + s, tm, stride=TOP_K * SUBLANES), :])
            lo_acc = lo_acc + wk[k] * lo
            hi_acc = hi_acc + wk[k] * hi
        y_scr[:, s * LANES:(s + 1) * LANES] = lo_acc
        y_scr[:, half + s * LANES: half + (s + 1) * LANES] = hi_acc
    x = _unpack_rows(u_ref, tm).astype(BF16)
    hidden = (_silu(_dot(x, wsg_ref[...])) * _dot(x, wsu_ref[...])).astype(BF16)
    y = y_scr[...] + _dot(hidden, wsd_ref[...])
    z = DN_ALPHA * x1_ref[...] + g2_ref[...] * y
    o_ref[...] = _layer_norm(z, lng_ref[...], lnb_ref[...])


def _combine(yg, wts_t, u2p, x1, mod3, wsg, wsu, wsd, ln_g, ln_b, seq, tm):
    t, d = x1.shape
    f = wsg.shape[1]
    tiles_per_seq = seq // tm
    kern = functools.partial(_combine_kernel, tm=tm)
    return pl.pallas_call(
        kern,
        grid=(t // tm,),
        in_specs=[
            pl.BlockSpec((tm * TOP_K * SUBLANES, LANES), lambda i: (i, 0)),
            pl.BlockSpec((tm, TOP_K), lambda i: (i, 0)),
            pl.BlockSpec((tm * SUBLANES, LANES), lambda i: (i, 0)),
            pl.BlockSpec((tm, d), lambda i: (i, 0)),
            pl.BlockSpec((None, 1, d), lambda i: ((i // tiles_per_seq) * 6 + 5, 0, 0)),
            pl.BlockSpec((d, f), lambda i: (0, 0)),
            pl.BlockSpec((d, f), lambda i: (0, 0)),
            pl.BlockSpec((f, d), lambda i: (0, 0)),
            pl.BlockSpec((1, d), lambda i: (0, 0)),
            pl.BlockSpec((1, d), lambda i: (0, 0)),
        ],
        out_specs=pl.BlockSpec((tm, d), lambda i: (i, 0)),
        out_shape=jax.ShapeDtypeStruct((t, d), F32),
        scratch_shapes=[pltpu.VMEM((tm, d), F32)],
        compiler_params=_cparams(("arbitrary",)),
        name="combine",
    )(yg, wts_t, u2p, x1, mod3, wsg, wsu, wsd, ln_g.reshape(1, d), ln_b.reshape(1, d))


def _pick_tile(n, pref):
    while n % pref:
        pref //= 2
    return pref


def _prep_w_in(w_in):
    d = w_in.shape[0]
    o = 0
    rq = w_in[:, o:o + RET_QK_W]; o += RET_QK_W
    rk = w_in[:, o:o + RET_QK_W]; o += RET_QK_W
    rv = w_in[:, o:o + RET_V_W]; o += RET_V_W
    rg = w_in[:, o:o + RET_V_W]; o += RET_V_W
    mqk = w_in[:, o:o + 2 * ML_QK_W]; o += 2 * ML_QK_W
    mv = w_in[:, o:o + ML_V_W]; o += ML_V_W
    mo = w_in[:, o:o + ML_V_W]; o += ML_V_W
    gates = w_in[:, o:o + 2 * ML_HEADS]

    def split_pairs(w):
        return w.reshape(d, RET_HEADS, RET_DK // 2, 2).transpose(0, 1, 3, 2).reshape(d, RET_QK_W)

    gates_pad = jnp.pad(gates, ((0, 0), (0, MXU_N - 2 * ML_HEADS)))
    return jnp.concatenate([split_pairs(rq), split_pairs(rk), rv, mv, rg, mqk, mo, gates_pad],
                           axis=1).astype(BF16)


def _layer(x, c, w_ada, b_ada, w_in, b_gates, conv_w, conv_b, ret_norm_w, ml_norm_w, w_out,
           ln1_g, ln1_b, w_router, e_bias, w_gate, w_up, w_down, ws_gate, ws_up, ws_down,
           ln2_g, ln2_b, expert_rows, gather_rows):
    bsz, seq, d = x.shape
    t = bsz * seq
    x2d = x.reshape(t, d)

    mod = _ada_mod(c, w_ada, b_ada)
    mod3 = mod.reshape(bsz * 6, 1, d)

    inv = ROPE_BASE ** (-jnp.arange(0, RET_DK, 2, dtype=F32) / RET_DK)
    ang = jnp.arange(seq, dtype=F32)[:, None] * inv
    pb, pf = _in_proj(x2d, mod3, _prep_w_in(w_in), jnp.cos(ang), jnp.sin(ang), seq,
                      _pick_tile(seq, 512))

    ts = _pick_tile(seq, 256)
    r_mix = _retention(pb, pf, ret_norm_w, bsz, seq, ts)
    h_mix = _mlstm(pb, pf, b_gates, conv_w, conv_b, ml_norm_w, bsz, seq, ts)

    x1, u2p = _out_proj(r_mix, h_mix, w_out.astype(BF16), x2d, mod3, ln1_g, ln1_b, seq,
                        _pick_tile(seq, 256))

    eidx, wts, rank, counts = _router(u2p, w_router, e_bias, t, _pick_tile(t, 512))

    g = expert_rows
    a = t * TOP_K
    n_blocks = -(-(a + N_EXPERTS * (g - 1)) // g)
    n_blocks = -(-n_blocks * g // gather_rows) * gather_rows // g
    padded = (counts + g - 1) // g * g
    pad_end = jnp.cumsum(padded)
    pad_start = pad_end - padded
    dest = pad_start[eidx] + rank
    n_used = (pad_end[-1] // g).astype(I32)
    blk = jnp.arange(n_blocks, dtype=I32)
    block_e = jnp.minimum(jnp.searchsorted(pad_end, jnp.minimum(blk, n_used - 1) * g, side='right'),
                          N_EXPERTS - 1).astype(I32)
    tok = jnp.broadcast_to(jnp.arange(t, dtype=I32)[None, :], (TOP_K, t))
    buf_tok = jnp.zeros((n_blocks * g,), I32).at[dest.reshape(-1)].set(tok.reshape(-1))

    used_steps = (n_used * g + gather_rows - 1) // gather_rows
    xs = _row_gather(u2p, buf_tok, used_steps.reshape(1).astype(I32), gather_rows)
    ys = _experts(xs, block_e, n_used.reshape(1), w_gate.astype(BF16), w_up.astype(BF16),
                  w_down.astype(BF16), g)
    dest_t = dest.T.reshape(-1).astype(I32)
    yg = _row_gather(ys, dest_t, jnp.full((1,), a // gather_rows, I32), gather_rows)

    out = _combine(yg, wts.T, u2p, x1, mod3, ws_gate.astype(BF16), ws_up.astype(BF16),
                   ws_down.astype(BF16), ln2_g, ln2_b, seq, _pick_tile(seq, 256))
    return out.reshape(bsz, seq, d)


def kernel(x, c, w_ada, b_ada, w_in, b_gates, conv_w, conv_b, ret_norm_w, ml_norm_w, w_out, ln1_g,
           ln1_b, w_router, e_bias, w_gate, w_up, w_down, ws_gate, ws_up, ws_down, ln2_g, ln2_b):
    for l in range(DEPTH):
        x = _layer(x, c, w_ada[l], b_ada[l], w_in[l], b_gates[l], conv_w[l], conv_b[l],
                   ret_norm_w[l], ml_norm_w[l], w_out[l], ln1_g[l], ln1_b[l], w_router[l],
                   e_bias[l], w_gate[l], w_up[l], w_down[l], ws_gate[l], ws_up[l], ws_down[l],
                   ln2_g[l], ln2_b[l], EXPERT_ROWS, GATHER_ROWS)
    return x
```

```python
import functools
import math

import numpy as np
import jax
import jax.numpy as jnp
from jax import lax
from jax.experimental import pallas as pl
from jax.experimental.pallas import tpu as pltpu

F32 = jnp.float32
BF16 = jnp.bfloat16
I32 = jnp.int32

DEPTH = 1
CHUNK = 64
RET_HEADS = 4
RET_DK = 256
RET_DV = 256
ML_HEADS = 4
ML_DK = 128
ML_DV = 256
CONV_W = 4
N_EXPERTS = 64
TOP_K = 8
N_GROUPS = 8
TOPK_GROUPS = 4
GROUP_SIZE = N_EXPERTS // N_GROUPS
ROUTED_SCALE = 2.5
ROPE_BASE = 10000.0
LN_EPS = 1e-5
GN_EPS = 1e-6
DN_ALPHA = (2 * DEPTH) ** 0.25

RET_QK_W = RET_HEADS * RET_DK
RET_V_W = RET_HEADS * RET_DV
ML_QK_W = ML_HEADS * ML_DK
ML_V_W = ML_HEADS * ML_DV

LANES = 128
SUBLANES = 8
D_MODEL = 2048
ROW_SUB = D_MODEL // LANES
MXU_N = 256
VMEM_LIMIT = 56 * 1024 * 1024

EXPERT_ROWS = 512
DISPATCH_ROWS = 512


def _cparams(sem):
    return pltpu.CompilerParams(dimension_semantics=sem, vmem_limit_bytes=VMEM_LIMIT)


def _dot(a, b):
    return jnp.dot(a, b, preferred_element_type=F32)


def _dot_nt(a, b):
    return lax.dot_general(a, b, (((1,), (1,)), ((), ())), preferred_element_type=F32)


def _dot_tn(a, b):
    return lax.dot_general(a, b, (((0,), (0,)), ((), ())), preferred_element_type=F32)


def _sigmoid(x):
    return 1.0 / (1.0 + jnp.exp(-x))


def _silu(x):
    return x * _sigmoid(x)


def _pack_rows(ref, val):
    n = val.shape[0]
    for s in range(ROW_SUB):
        ref[pl.ds(s, n, stride=ROW_SUB), :] = val[:, s * LANES:(s + 1) * LANES]


def _unpack_rows(ref, n):
    return jnp.concatenate([ref[pl.ds(s, n, stride=ROW_SUB), :] for s in range(ROW_SUB)], axis=1)


def _ada_kernel(c_ref, w_ref, b_ref, o_ref):
    s = _silu(c_ref[...]).astype(BF16)
    o_ref[...] = _dot(s, w_ref[...].astype(BF16)) + b_ref[...]


def _ada_mod(c, w_ada, b_ada):
    bsz, d = c.shape
    n = w_ada.shape[1]
    tn = 512
    return pl.pallas_call(
        _ada_kernel,
        grid=(n // tn,),
        in_specs=[pl.BlockSpec((bsz, d), lambda j: (0, 0)),
                  pl.BlockSpec((d, tn), lambda j: (0, j)),
                  pl.BlockSpec((1, tn), lambda j: (0, j))],
        out_specs=pl.BlockSpec((bsz, tn), lambda j: (0, j)),
        out_shape=jax.ShapeDtypeStruct((bsz, n), F32),
        compiler_params=_cparams(("arbitrary",)),
        name="ada_mod",
    )(c, w_ada, b_ada.reshape(1, n))


def _inproj_kernel(x_ref, sc_ref, sh_ref, w_ref, cos_ref, sin_ref, pb_ref, pf_ref, u_scr,
                   *, n_rot, n_b):
    j = pl.program_id(1)

    @pl.when(j == 0)
    def _():
        u_scr[...] = (x_ref[...] * (1.0 + sc_ref[...]) + sh_ref[...]).astype(BF16)

    acc = _dot(u_scr[...], w_ref[...])

    @pl.when(j < n_rot)
    def _():
        half = acc.shape[1] // 2
        x1 = acc[:, :half]
        x2 = acc[:, half:]
        c = cos_ref[...]
        s = sin_ref[...]
        scale = jnp.where(j < n_rot // 2, 1.0, RET_DK ** -0.5).astype(F32)
        o = jnp.concatenate([x1 * c - x2 * s, x1 * s + x2 * c], axis=1) * scale
        pb_ref[...] = o.astype(BF16)

    @pl.when((j >= n_rot) & (j < n_b))
    def _():
        pb_ref[...] = acc.astype(BF16)

    @pl.when(j >= n_b)
    def _():
        pf_ref[...] = acc


def _in_proj(x2d, mod3, w_all, cos_t, sin_t, seq, tm):
    t, d = x2d.shape
    n_all = w_all.shape[1]
    tn = MXU_N
    n_tiles = n_all // tn
    n_rot = 2 * RET_QK_W // tn
    n_b = (2 * RET_QK_W + RET_V_W + ML_V_W) // tn
    n_f = n_tiles - n_b
    tiles_per_seq = seq // tm
    kern = functools.partial(_inproj_kernel, n_rot=n_rot, n_b=n_b)
    return pl.pallas_call(
        kern,
        grid=(t // tm, n_tiles),
        in_specs=[
            pl.BlockSpec((tm, d), lambda i, j: (i, 0)),
            pl.BlockSpec((None, 1, d), lambda i, j: ((i // tiles_per_seq) * 6 + 1, 0, 0)),
            pl.BlockSpec((None, 1, d), lambda i, j: ((i // tiles_per_seq) * 6 + 0, 0, 0)),
            pl.BlockSpec((d, tn), lambda i, j: (0, j)),
            pl.BlockSpec((tm, tn // 2), lambda i, j: (i % tiles_per_seq, 0)),
            pl.BlockSpec((tm, tn // 2), lambda i, j: (i % tiles_per_seq, 0)),
        ],
        out_specs=[
            pl.BlockSpec((tm, tn), lambda i, j: (i, jnp.minimum(j, n_b - 1))),
            pl.BlockSpec((tm, tn), lambda i, j: (i, jnp.maximum(j - n_b, 0))),
        ],
        out_shape=[jax.ShapeDtypeStruct((t, n_b * tn), BF16),
                   jax.ShapeDtypeStruct((t, n_f * tn), F32)],
        scratch_shapes=[pltpu.VMEM((tm, d), BF16)],
        compiler_params=_cparams(("arbitrary", "arbitrary")),
        name="in_proj",
    )(x2d, mod3, mod3, w_all, cos_t, sin_t)


def _head_norm(h, w):
    mu = jnp.mean(h, axis=-1, keepdims=True)
    hc = h - mu
    var = jnp.mean(hc * hc, axis=-1, keepdims=True)
    return hc * lax.rsqrt(var + GN_EPS) * w


def _retention_kernel(q_ref, k_ref, v_ref, g_ref, nw_ref, o_ref, state_ref, *, n_chunks):
    L = CHUNK

    @pl.when(pl.program_id(1) == 0)
    def _():
        state_ref[...] = jnp.zeros_like(state_ref)

    ii = lax.broadcasted_iota(I32, (L, L), 0)
    jj = lax.broadcasted_iota(I32, (L, L), 1)
    diff = (ii - jj).astype(F32)
    causal = ii >= jj
    row = lax.broadcasted_iota(I32, (L, 1), 0).astype(F32)
    consts = []
    for h in range(RET_HEADS):
        log_g = math.log1p(-(2.0 ** (-5.0 - h)))
        intra = jnp.where(causal, jnp.exp(log_g * jnp.maximum(diff, 0.0)), 0.0)
        q_decay = jnp.exp(log_g * (row + 1.0))
        k_decay = jnp.exp(log_g * (L - 1.0 - row))
        consts.append((intra, q_decay, k_decay, math.exp(log_g * L)))

    def chunk_body(c, carry):
        rows = pl.ds(pl.multiple_of(c * L, L), L)
        for h in range(RET_HEADS):
            intra, q_decay, k_decay, chunk_decay = consts[h]
            qs = slice(h * RET_DK, (h + 1) * RET_DK)
            vs = slice(h * RET_DV, (h + 1) * RET_DV)
            qb = q_ref[rows, qs]
            kb = k_ref[rows, qs]
            vb = v_ref[rows, vs]
            st = state_ref[h]
            s = _dot_nt(qb, kb) * intra
            out = _dot(s.astype(BF16), vb) + q_decay * _dot(qb, st.astype(BF16))
            kd = (kb.astype(F32) * k_decay).astype(BF16)
            state_ref[h] = chunk_decay * st + _dot_tn(kd, vb)
            r = _head_norm(out, nw_ref[:, vs]) * _silu(g_ref[rows, vs])
            o_ref[rows, vs] = r.astype(BF16)
        return carry

    lax.fori_loop(0, n_chunks, chunk_body, 0)


def _retention(pb, pf, norm_w, bsz, seq, ts):
    t = pb.shape[0]
    n_s = seq // ts
    kern = functools.partial(_retention_kernel, n_chunks=ts // CHUNK)
    blk = lambda col: pl.BlockSpec((ts, RET_QK_W), lambda b, s: (b * n_s + s, col))
    return pl.pallas_call(
        kern,
        grid=(bsz, n_s),
        in_specs=[blk(0), blk(1), blk(2), blk(0),
                  pl.BlockSpec((1, RET_V_W), lambda b, s: (0, 0))],
        out_specs=pl.BlockSpec((ts, RET_V_W), lambda b, s: (b * n_s + s, 0)),
        out_shape=jax.ShapeDtypeStruct((t, RET_V_W), BF16),
        scratch_shapes=[pltpu.VMEM((RET_HEADS, RET_DK, RET_DV), F32)],
        compiler_params=_cparams(("arbitrary", "arbitrary")),
        name="retention",
    )(pb, pb, pb, pf, norm_w.reshape(1, RET_V_W))


def _split_hi_lo(x):
    hi = x.astype(BF16)
    lo = (x - hi.astype(F32)).astype(BF16)
    return hi, lo


def _mlstm_kernel(qk_ref, v_ref, o_ref, gt_ref, bg_ref, cw_ref, cb_ref, nw_ref, out_ref,
                  ext_ref, act_ref, c_ref, n_ref, m_ref, *, ts):
    L = CHUNK
    n_chunks = ts // L
    pad = SUBLANES

    @pl.when(pl.program_id(1) == 0)
    def _():
        ext_ref[0:pad, :] = jnp.zeros((pad, 2 * ML_QK_W), F32)
        c_ref[...] = jnp.zeros_like(c_ref)
        n_ref[...] = jnp.zeros_like(n_ref)
        m_ref[...] = jnp.zeros_like(m_ref)

    ext_ref[pad:pad + ts, :] = qk_ref[...]
    conv = cb_ref[...] + jnp.zeros((ts, 2 * ML_QK_W), F32)
    for w in range(CONV_W):
        off = pad - (CONV_W - 1) + w
        conv = conv + ext_ref[off:off + ts, :] * cw_ref[w:w + 1, :]
    act_ref[...] = _silu(conv)
    ext_ref[0:pad, :] = qk_ref[ts - pad:ts, :]

    g = gt_ref[...] + bg_ref[...]
    lf = jnp.minimum(g, 0.0) - jnp.log(1.0 + jnp.exp(-jnp.abs(g)))
    ti = lax.broadcasted_iota(I32, (ts, ts), 0)
    tj = lax.broadcasted_iota(I32, (ts, ts), 1)
    shift = L.bit_length() - 1
    tri = jnp.where((ti >= tj) & ((ti >> shift) == (tj >> shift)), 1.0, 0.0).astype(BF16)
    lf_hi, lf_lo = _split_hi_lo(lf)
    bcum = _dot(tri, lf_hi) + _dot(tri, lf_lo)
    rvec = g - pltpu.roll(bcum, LANES - ML_HEADS, 1)

    ii = lax.broadcasted_iota(I32, (L, L), 0)
    jj = lax.broadcasted_iota(I32, (L, L), 1)
    causal = ii >= jj
    lane = lax.broadcasted_iota(I32, (L, LANES), 1)

    for c in range(n_chunks):
        rows = slice(c * L, (c + 1) * L)
        r_hi, r_lo = _split_hi_lo(rvec[rows, :])
        for h in range(ML_HEADS):
            sel = jnp.where(lane == h, 1.0, 0.0).astype(BF16)
            rmat = _dot_nt(sel, r_hi) + _dot_nt(sel, r_lo)
            r_col = rvec[rows, h:h + 1]
            b_col = bcum[rows, ML_HEADS + h:ML_HEADS + h + 1]
            m_old = m_ref[h:h + 1, 0:1]
            cmax = jnp.max(jnp.where(causal, rmat, -jnp.inf), axis=1, keepdims=True)
            big_m = jnp.maximum(m_old, cmax)
            dmat = jnp.where(causal, jnp.exp(rmat - big_m), 0.0)
            w_inter = jnp.exp(m_old - big_m)
            floor = jnp.exp(-(b_col + big_m))

            qs = slice(h * ML_DK, (h + 1) * ML_DK)
            ks = slice(ML_QK_W + h * ML_DK, ML_QK_W + (h + 1) * ML_DK)
            vs = slice(h * ML_DV, (h + 1) * ML_DV)
            qf = act_ref[rows, qs] * (ML_DK ** -0.5)
            kf = act_ref[rows, ks]
            qb = qf.astype(BF16)
            vb = v_ref[rows, vs]
            cst = c_ref[h]
            nst = n_ref[h:h + 1, :]

            s = _dot_nt(qb, kf.astype(BF16)) * dmat
            num = _dot(s.astype(BF16), vb) + w_inter * _dot(qb, cst.astype(BF16))
            qn = jnp.sum(qb.astype(F32) * nst, axis=1, keepdims=True)
            den = jnp.sum(s, axis=1, keepdims=True) + w_inter * qn
            hout = num * (1.0 / jnp.maximum(jnp.abs(den), floor))

            m_last = big_m[L - 1:L, :]
            m_new = b_col[L - 1:L, :] + m_last
            g_state = jnp.exp(m_old - m_last)
            g_k = jnp.exp(r_col - m_last)
            kw = kf * g_k
            c_ref[h] = g_state * cst + _dot_tn(kw.astype(BF16), vb)
            n_ref[h:h + 1, :] = g_state * nst + jnp.sum(kw, axis=0, keepdims=True)
            m_ref[h:h + 1, :] = jnp.broadcast_to(m_new, (1, LANES))

            y = _head_norm(hout, nw_ref[:, vs]) * _sigmoid(o_ref[rows, vs])
            out_ref[rows, vs] = y.astype(BF16)


def _mlstm(pb, pf, b_gates, conv_w, conv_b, norm_w, bsz, seq, ts):
    t = pb.shape[0]
    n_s = seq // ts
    bg = jnp.zeros((1, LANES), F32).at[0, :2 * ML_HEADS].set(b_gates)
    kern = functools.partial(_mlstm_kernel, ts=ts)
    wide = 2 * ML_QK_W
    gate_blk = (RET_V_W + wide + ML_V_W) // LANES
    return pl.pallas_call(
        kern,
        grid=(bsz, n_s),
        in_specs=[
            pl.BlockSpec((ts, wide), lambda b, s: (b * n_s + s, RET_V_W // wide)),
            pl.BlockSpec((ts, ML_V_W), lambda b, s: (b * n_s + s, 3)),
            pl.BlockSpec((ts, ML_V_W), lambda b, s: (b * n_s + s, (RET_V_W + wide) // ML_V_W)),
            pl.BlockSpec((ts, LANES), lambda b, s: (b * n_s + s, gate_blk)),
            pl.BlockSpec((1, LANES), lambda b, s: (0, 0)),
            pl.BlockSpec((CONV_W, wide), lambda b, s: (0, 0)),
            pl.BlockSpec((1, wide), lambda b, s: (0, 0)),
            pl.BlockSpec((1, ML_V_W), lambda b, s: (0, 0)),
        ],
        out_specs=pl.BlockSpec((ts, ML_V_W), lambda b, s: (b * n_s + s, 0)),
        out_shape=jax.ShapeDtypeStruct((t, ML_V_W), BF16),
        scratch_shapes=[
            pltpu.VMEM((ts + SUBLANES, wide), F32),
            pltpu.VMEM((ts, wide), F32),
            pltpu.VMEM((ML_HEADS, ML_DK, ML_DV), F32),
            pltpu.VMEM((SUBLANES, ML_DK), F32),
            pltpu.VMEM((SUBLANES, LANES), F32),
        ],
        compiler_params=_cparams(("arbitrary", "arbitrary")),
        name="mlstm",
    )(pf, pb, pf, pf, bg, conv_w, conv_b.reshape(1, wide), norm_w.reshape(1, ML_V_W))


def _layer_norm(z, g, b):
    mu = jnp.mean(z, axis=-1, keepdims=True)
    zc = z - mu
    var = jnp.mean(zc * zc, axis=-1, keepdims=True)
    return zc * lax.rsqrt(var + LN_EPS) * g + b


def _outproj_kernel(r_ref, h_ref, w1_ref, w2_ref, x_ref, g1_ref, sc_ref, sh_ref, lng_ref, lnb_ref,
                    x1_ref, u2_ref):
    mix = _dot(r_ref[...], w1_ref[...]) + _dot(h_ref[...], w2_ref[...])
    z = DN_ALPHA * x_ref[...] + g1_ref[...] * mix
    x1 = _layer_norm(z, lng_ref[...], lnb_ref[...])
    x1_ref[...] = x1
    _pack_rows(u2_ref, x1 * (1.0 + sc_ref[...]) + sh_ref[...])


def _out_proj(r_mix, h_mix, w_out_bf, x2d, mod3, ln_g, ln_b, seq, tm):
    t, d = x2d.shape
    tiles_per_seq = seq // tm
    modblk = lambda k: pl.BlockSpec((None, 1, d), lambda i: ((i // tiles_per_seq) * 6 + k, 0, 0))
    return pl.pallas_call(
        _outproj_kernel,
        grid=(t // tm,),
        in_specs=[
            pl.BlockSpec((tm, RET_V_W), lambda i: (i, 0)),
            pl.BlockSpec((tm, ML_V_W), lambda i: (i, 0)),
            pl.BlockSpec((RET_V_W, d), lambda i: (0, 0)),
            pl.BlockSpec((ML_V_W, d), lambda i: (1, 0)),
            pl.BlockSpec((tm, d), lambda i: (i, 0)),
            modblk(2), modblk(4), modblk(3),
            pl.BlockSpec((1, d), lambda i: (0, 0)),
            pl.BlockSpec((1, d), lambda i: (0, 0)),
        ],
        out_specs=[pl.BlockSpec((tm, d), lambda i: (i, 0)),
                   pl.BlockSpec((tm * ROW_SUB, LANES), lambda i: (i, 0))],
        out_shape=[jax.ShapeDtypeStruct((t, d), F32),
                   jax.ShapeDtypeStruct((t * ROW_SUB, LANES), F32)],
        compiler_params=_cparams(("arbitrary",)),
        name="out_proj",
    )(r_mix, h_mix, w_out_bf, w_out_bf, x2d, mod3, mod3, mod3,
      ln_g.reshape(1, d), ln_b.reshape(1, d))


def _router_kernel(u_ref, w_ref, eb_ref, eidx_ref, wts_ref, rank_ref, cnt_ref, carry_ref, *, tm):
    @pl.when(pl.program_id(0) == 0)
    def _():
        carry_ref[...] = jnp.zeros_like(carry_ref)

    x = _unpack_rows(u_ref, tm).astype(BF16)
    logits = _dot_nt(w_ref[...], x)
    scores = _sigmoid(logits)
    biased = scores + eb_ref[:, 0:1]
    neg = -jnp.inf
    gi = lax.broadcasted_iota(I32, (N_GROUPS, tm), 0)
    sc = [scores[j * N_GROUPS:(j + 1) * N_GROUPS, :] for j in range(GROUP_SIZE)]
    bs = [biased[j * N_GROUPS:(j + 1) * N_GROUPS, :] for j in range(GROUP_SIZE)]
    eid = [gi * GROUP_SIZE + j for j in range(GROUP_SIZE)]

    m1 = functools.reduce(jnp.maximum, bs)
    first = functools.reduce(jnp.minimum,
                             [jnp.where(bs[j] == m1, j, GROUP_SIZE) for j in range(GROUP_SIZE)])
    m2 = functools.reduce(jnp.maximum,
                          [jnp.where(first == j, neg, bs[j]) for j in range(GROUP_SIZE)])
    grp = m1 + m2

    keep = jnp.zeros((N_GROUPS, tm), dtype=jnp.bool_)
    for _ in range(TOPK_GROUPS):
        top = jnp.max(grp, axis=0, keepdims=True)
        pick = jnp.min(jnp.where(grp == top, gi, N_GROUPS), axis=0, keepdims=True)
        hit = gi == pick
        keep = keep | hit
        grp = jnp.where(hit, neg, grp)

    cand = [jnp.where(keep, bs[j], neg) for j in range(GROUP_SIZE)]
    chosen = [jnp.zeros((N_GROUPS, tm), dtype=jnp.bool_) for _ in range(GROUP_SIZE)]
    picks, vals = [], []
    for _ in range(TOP_K):
        top = jnp.max(functools.reduce(jnp.maximum, cand), axis=0, keepdims=True)
        e_pick = jnp.min(functools.reduce(
            jnp.minimum, [jnp.where(cand[j] == top, eid[j], N_EXPERTS) for j in range(GROUP_SIZE)]),
            axis=0, keepdims=True)
        hits = [eid[j] == e_pick for j in range(GROUP_SIZE)]
        val = jnp.sum(functools.reduce(
            jnp.add, [jnp.where(hits[j], sc[j], 0.0) for j in range(GROUP_SIZE)]),
            axis=0, keepdims=True)
        cand = [jnp.where(hits[j], neg, cand[j]) for j in range(GROUP_SIZE)]
        chosen = [chosen[j] | hits[j] for j in range(GROUP_SIZE)]
        picks.append(e_pick)
        vals.append(val)

    total = functools.reduce(jnp.add, vals)
    inv = ROUTED_SCALE / total
    eidx_ref[...] = jnp.concatenate(picks, axis=0)
    wts_ref[...] = jnp.concatenate([v * inv for v in vals], axis=0)

    onehot = jnp.concatenate([jnp.where(chosen[j], 1.0, 0.0) for j in range(GROUP_SIZE)], axis=0)
    ti = lax.broadcasted_iota(I32, (tm, tm), 0)
    tj = lax.broadcasted_iota(I32, (tm, tm), 1)
    before = jnp.where(ti < tj, 1.0, 0.0).astype(BF16)
    carry = carry_ref[...]
    prior = _dot(onehot.astype(BF16), before) + carry[:, 0:1]
    pr = [prior[j * N_GROUPS:(j + 1) * N_GROUPS, :] for j in range(GROUP_SIZE)]
    ranks = []
    for k in range(TOP_K):
        ranks.append(jnp.sum(functools.reduce(
            jnp.add, [jnp.where(eid[j] == picks[k], pr[j], 0.0) for j in range(GROUP_SIZE)]),
            axis=0, keepdims=True))
    rank_ref[...] = jnp.concatenate(ranks, axis=0).astype(I32)
    carry = carry + jnp.sum(onehot, axis=1, keepdims=True)
    carry_ref[...] = carry
    cnt_ref[...] = carry.astype(I32)


def _router(u2p, w_router, e_bias, t, tm):
    d = w_router.shape[0]
    perm = np.arange(N_EXPERTS).reshape(N_GROUPS, GROUP_SIZE).T.reshape(-1)
    w_t = w_router.T[perm].astype(BF16)
    eb = jnp.broadcast_to(e_bias[perm][:, None], (N_EXPERTS, LANES)).astype(F32)
    kern = functools.partial(_router_kernel, tm=tm)
    eidx, wts, rank, cnt = pl.pallas_call(
        kern,
        grid=(t // tm,),
        in_specs=[pl.BlockSpec((tm * ROW_SUB, LANES), lambda i: (i, 0)),
                  pl.BlockSpec((N_EXPERTS, d), lambda i: (0, 0)),
                  pl.BlockSpec((N_EXPERTS, LANES), lambda i: (0, 0))],
        out_specs=[pl.BlockSpec((TOP_K, tm), lambda i: (0, i)),
                   pl.BlockSpec((TOP_K, tm), lambda i: (0, i)),
                   pl.BlockSpec((TOP_K, tm), lambda i: (0, i)),
                   pl.BlockSpec((N_EXPERTS, LANES), lambda i: (0, 0))],
        out_shape=[jax.ShapeDtypeStruct((TOP_K, t), I32),
                   jax.ShapeDtypeStruct((TOP_K, t), F32),
                   jax.ShapeDtypeStruct((TOP_K, t), I32),
                   jax.ShapeDtypeStruct((N_EXPERTS, LANES), I32)],
        scratch_shapes=[pltpu.VMEM((N_EXPERTS, LANES), F32)],
        compiler_params=_cparams(("arbitrary",)),
        name="router",
    )(u2p, w_t, eb)
    counts = jnp.zeros((N_EXPERTS,), I32).at[perm].set(cnt[:, 0])
    return eidx, wts, rank, counts


def _dispatch_kernel(cnt_ref, start_ref, dest_ref, u_ref, xs_ref, zero_ref, sem, pad_sem,
                     *, tm, block_rows):
    def row_dst(row):
        return xs_ref.at[pl.ds(pl.multiple_of(row * ROW_SUB, ROW_SUB), ROW_SUB)]

    @pl.when(pl.program_id(0) == 0)
    def _():
        zero_ref[...] = jnp.zeros_like(zero_ref)

        def per_expert(e, carry):
            lo = start_ref[e] + cnt_ref[e]
            hi = start_ref[e] + (cnt_ref[e] + block_rows - 1) // block_rows * block_rows

            def issue(r, c):
                pltpu.make_async_copy(zero_ref, row_dst(r), pad_sem).start()
                return c

            def drain(r, c):
                pltpu.make_async_copy(zero_ref, row_dst(r), pad_sem).wait()
                return c

            lax.fori_loop(lo, hi, issue, 0)
            lax.fori_loop(lo, hi, drain, 0)
            return carry

        lax.fori_loop(0, N_EXPERTS, per_expert, 0)

    def per_token(t, carry):
        src = u_ref.at[pl.ds(pl.multiple_of(t * ROW_SUB, ROW_SUB), ROW_SUB)]
        for k in range(TOP_K):
            pltpu.make_async_copy(src, row_dst(dest_ref[0, 0, t * TOP_K + k]), sem).start()
        return carry

    lax.fori_loop(0, tm, per_token, 0)
    for _ in range(TOP_K):
        pltpu.make_async_copy(u_ref, xs_ref.at[pl.ds(0, tm * ROW_SUB)], sem).wait()


def _dispatch(u2p, dest_t, counts, pad_start, n_rows, t, tm, block_rows):
    kern = functools.partial(_dispatch_kernel, tm=tm, block_rows=block_rows)
    grid_spec = pltpu.PrefetchScalarGridSpec(
        num_scalar_prefetch=2,
        grid=(t // tm,),
        in_specs=[pl.BlockSpec((1, 1, tm * TOP_K), lambda i, c, s: (i, 0, 0),
                               memory_space=pltpu.SMEM),
                  pl.BlockSpec((tm * ROW_SUB, LANES), lambda i, c, s: (i, 0))],
        out_specs=pl.BlockSpec(memory_space=pl.ANY),
        scratch_shapes=[pltpu.VMEM((ROW_SUB, LANES), F32),
                        pltpu.SemaphoreType.DMA, pltpu.SemaphoreType.DMA],
    )
    return pl.pallas_call(
        kern,
        grid_spec=grid_spec,
        out_shape=jax.ShapeDtypeStruct((n_rows * ROW_SUB, LANES), F32),
        compiler_params=_cparams(("arbitrary",)),
        name="dispatch",
    )(counts, pad_start, dest_t.reshape(t // tm, 1, tm * TOP_K), u2p)


def _expert_kernel(be_ref, nu_ref, xs_ref, wg_ref, wu_ref, wd_ref, ys_ref, *, rows):
    @pl.when(pl.program_id(0) < nu_ref[0])
    def _():
        x = _unpack_rows(xs_ref, rows).astype(BF16)
        gate = _dot(x, wg_ref[...])
        up = _dot(x, wu_ref[...])
        hidden = (_silu(gate) * up).astype(BF16)
        _pack_rows(ys_ref, _dot(hidden, wd_ref[...]))


def _experts(xs, block_e, n_used, wg, wu, wd, rows):
    n_blocks = block_e.shape[0]
    _, d, f = wg.shape
    kern = functools.partial(_expert_kernel, rows=rows)
    row_blk = pl.BlockSpec((rows * ROW_SUB, LANES),
                           lambda i, be, nu: (jnp.minimum(i, nu[0] - 1), 0))
    grid_spec = pltpu.PrefetchScalarGridSpec(
        num_scalar_prefetch=2,
        grid=(n_blocks,),
        in_specs=[row_blk,
                  pl.BlockSpec((None, d, f), lambda i, be, nu: (be[i], 0, 0)),
                  pl.BlockSpec((None, d, f), lambda i, be, nu: (be[i], 0, 0)),
                  pl.BlockSpec((None, f, d), lambda i, be, nu: (be[i], 0, 0))],
        out_specs=row_blk,
    )
    return pl.pallas_call(
        kern,
        grid_spec=grid_spec,
        out_shape=jax.ShapeDtypeStruct(xs.shape, F32),
        compiler_params=_cparams(("arbitrary",)),
        name="experts",
    )(block_e, n_used, xs, wg, wu, wd)


def _combine_kernel(dcur_ref, dnext_ref, ys_ref, w_ref, u_ref, x1_ref, g2_ref, wsg_ref, wsu_ref,
                    wsd_ref, lng_ref, lnb_ref, o_ref, ybuf, y_scr, sem, *, tm, n_tiles):
    i = pl.program_id(0)
    slot = i % 2

    def issue(dref, into):
        def per_token(t, carry):
            for k in range(TOP_K):
                r = t * TOP_K + k
                pltpu.make_async_copy(
                    ys_ref.at[pl.ds(pl.multiple_of(dref[0, 0, r] * ROW_SUB, ROW_SUB), ROW_SUB)],
                    ybuf.at[into, pl.ds(pl.multiple_of(r * ROW_SUB, ROW_SUB), ROW_SUB)],
                    sem.at[into]).start()
            return carry

        lax.fori_loop(0, tm, per_token, 0)

    @pl.when(i == 0)
    def _():
        issue(dcur_ref, 0)

    @pl.when(i + 1 < n_tiles)
    def _():
        issue(dnext_ref, 1 - slot)

    pltpu.make_async_copy(ys_ref.at[pl.ds(0, tm * TOP_K * ROW_SUB)], ybuf.at[slot],
                          sem.at[slot]).wait()
    yg_ref = ybuf.at[slot]

    wk = [jnp.broadcast_to(w_ref[:, k:k + 1], (tm, LANES)) for k in range(TOP_K)]
    for s in range(ROW_SUB):
        acc = jnp.zeros((tm, LANES), F32)
        for k in range(TOP_K):
            acc = acc + wk[k] * yg_ref[pl.ds(k * ROW_SUB + s, tm, stride=TOP_K * ROW_SUB), :]
        y_scr[:, s * LANES:(s + 1) * LANES] = acc
    x = _unpack_rows(u_ref, tm).astype(BF16)
    hidden = (_silu(_dot(x, wsg_ref[...])) * _dot(x, wsu_ref[...])).astype(BF16)
    y = y_scr[...] + _dot(hidden, wsd_ref[...])
    z = DN_ALPHA * x1_ref[...] + g2_ref[...] * y
    o_ref[...] = _layer_norm(z, lng_ref[...], lnb_ref[...])


def _combine(ys, dest_t, wts_t, u2p, x1, mod3, wsg, wsu, wsd, ln_g, ln_b, seq, tm):
    t, d = x1.shape
    f = wsg.shape[1]
    tiles_per_seq = seq // tm
    n_tiles = t // tm
    kern = functools.partial(_combine_kernel, tm=tm, n_tiles=n_tiles)
    dest3 = dest_t.reshape(n_tiles, 1, tm * TOP_K)
    return pl.pallas_call(
        kern,
        grid=(n_tiles,),
        in_specs=[
            pl.BlockSpec((1, 1, tm * TOP_K), lambda i: (i, 0, 0), memory_space=pltpu.SMEM),
            pl.BlockSpec((1, 1, tm * TOP_K), lambda i: (jnp.minimum(i + 1, n_tiles - 1), 0, 0),
                         memory_space=pltpu.SMEM),
            pl.BlockSpec(memory_space=pl.ANY),
            pl.BlockSpec((tm, TOP_K), lambda i: (i, 0)),
            pl.BlockSpec((tm * ROW_SUB, LANES), lambda i: (i, 0)),
            pl.BlockSpec((tm, d), lambda i: (i, 0)),
            pl.BlockSpec((None, 1, d), lambda i: ((i // tiles_per_seq) * 6 + 5, 0, 0)),
            pl.BlockSpec((d, f), lambda i: (0, 0)),
            pl.BlockSpec((d, f), lambda i: (0, 0)),
            pl.BlockSpec((f, d), lambda i: (0, 0)),
            pl.BlockSpec((1, d), lambda i: (0, 0)),
            pl.BlockSpec((1, d), lambda i: (0, 0)),
        ],
        out_specs=pl.BlockSpec((tm, d), lambda i: (i, 0)),
        out_shape=jax.ShapeDtypeStruct((t, d), F32),
        scratch_shapes=[pltpu.VMEM((2, tm * TOP_K * ROW_SUB, LANES), F32),
                        pltpu.VMEM((tm, d), F32),
                        pltpu.SemaphoreType.DMA((2,))],
        compiler_params=_cparams(("arbitrary",)),
        name="combine",
    )(dest3, dest3, ys, wts_t, u2p, x1, mod3, wsg, wsu, wsd, ln_g.reshape(1, d), ln_b.reshape(1, d))


def _pick_tile(n, pref):
    while n % pref:
        pref //= 2
    return pref


def _prep_w_in(w_in):
    d = w_in.shape[0]
    o = 0
    rq = w_in[:, o:o + RET_QK_W]; o += RET_QK_W
    rk = w_in[:, o:o + RET_QK_W]; o += RET_QK_W
    rv = w_in[:, o:o + RET_V_W]; o += RET_V_W
    rg = w_in[:, o:o + RET_V_W]; o += RET_V_W
    mqk = w_in[:, o:o + 2 * ML_QK_W]; o += 2 * ML_QK_W
    mv = w_in[:, o:o + ML_V_W]; o += ML_V_W
    mo = w_in[:, o:o + ML_V_W]; o += ML_V_W
    gates = w_in[:, o:o + 2 * ML_HEADS]

    def split_pairs(w):
        return w.reshape(d, RET_HEADS, RET_DK // 2, 2).transpose(0, 1, 3, 2).reshape(d, RET_QK_W)

    gates_pad = jnp.pad(gates, ((0, 0), (0, MXU_N - 2 * ML_HEADS)))
    return jnp.concatenate([split_pairs(rq), split_pairs(rk), rv, mv, rg, mqk, mo, gates_pad],
                           axis=1).astype(BF16)


def _layer(x, c, w_ada, b_ada, w_in, b_gates, conv_w, conv_b, ret_norm_w, ml_norm_w, w_out,
           ln1_g, ln1_b, w_router, e_bias, w_gate, w_up, w_down, ws_gate, ws_up, ws_down,
           ln2_g, ln2_b, expert_rows, dispatch_rows):
    bsz, seq, d = x.shape
    t = bsz * seq
    x2d = x.reshape(t, d)

    mod = _ada_mod(c, w_ada, b_ada)
    mod3 = mod.reshape(bsz * 6, 1, d)

    inv = ROPE_BASE ** (-jnp.arange(0, RET_DK, 2, dtype=F32) / RET_DK)
    ang = jnp.arange(seq, dtype=F32)[:, None] * inv
    pb, pf = _in_proj(x2d, mod3, _prep_w_in(w_in), jnp.cos(ang), jnp.sin(ang), seq,
                      _pick_tile(seq, 512))

    ts = _pick_tile(seq, 256)
    r_mix = _retention(pb, pf, ret_norm_w, bsz, seq, ts)
    h_mix = _mlstm(pb, pf, b_gates, conv_w, conv_b, ml_norm_w, bsz, seq, ts)

    x1, u2p = _out_proj(r_mix, h_mix, w_out.astype(BF16), x2d, mod3, ln1_g, ln1_b, seq,
                        _pick_tile(seq, 256))

    eidx, wts, rank, counts = _router(u2p, w_router, e_bias, t, _pick_tile(t, 512))

    g = expert_rows
    a = t * TOP_K
    n_blocks = -(-(a + N_EXPERTS * (g - 1)) // g)
    padded = (counts + g - 1) // g * g
    pad_end = jnp.cumsum(padded).astype(I32)
    pad_start = pad_end - padded
    experts = jnp.arange(N_EXPERTS, dtype=I32)[:, None, None]
    dest = jnp.sum(jnp.where(eidx[None] == experts, pad_start[:, None, None], 0), axis=0) + rank
    dest_t = dest.T.reshape(-1).astype(I32)
    n_used = pad_end[-1] // g
    first_row = jnp.minimum(jnp.arange(n_blocks, dtype=I32), n_used - 1) * g
    block_e = jnp.minimum(jnp.sum((pad_end[None, :] <= first_row[:, None]).astype(I32), axis=1),
                          N_EXPERTS - 1)

    xs = _dispatch(u2p, dest_t, counts, pad_start, n_blocks * g, t, _pick_tile(t, dispatch_rows), g)
    ys = _experts(xs, block_e, n_used.reshape(1), w_gate.astype(BF16), w_up.astype(BF16),
                  w_down.astype(BF16), g)
    out = _combine(ys, dest_t, wts.T, u2p, x1, mod3, ws_gate.astype(BF16), ws_up.astype(BF16),
                   ws_down.astype(BF16), ln2_g, ln2_b, seq, _pick_tile(seq, 128))
    return out.reshape(bsz, seq, d)


def kernel(x, c, w_ada, b_ada, w_in, b_gates, conv_w, conv_b, ret_norm_w, ml_norm_w, w_out, ln1_g,
           ln1_b, w_router, e_bias, w_gate, w_up, w_down, ws_gate, ws_up, ws_down, ln2_g, ln2_b):
    for l in range(DEPTH):
        x = _layer(x, c, w_ada[l], b_ada[l], w_in[l], b_gates[l], conv_w[l], conv_b[l],
                   ret_norm_w[l], ml_norm_w[l], w_out[l], ln1_g[l], ln1_b[l], w_router[l],
                   e_bias[l], w_gate[l], w_up[l], w_down[l], ws_gate[l], ws_up[l], ws_down[l],
                   ln2_g[l], ln2_b[l], EXPERT_ROWS, DISPATCH_ROWS)
    return x
```

```python
import functools
import math

import numpy as np
import jax
import jax.numpy as jnp
from jax import lax
from jax.experimental import pallas as pl
from jax.experimental.pallas import tpu as pltpu

F32 = jnp.float32
BF16 = jnp.bfloat16
I32 = jnp.int32

DEPTH = 1
CHUNK = 64
RET_HEADS = 4
RET_DK = 256
RET_DV = 256
ML_HEADS = 4
ML_DK = 128
ML_DV = 256
CONV_W = 4
N_EXPERTS = 64
TOP_K = 8
N_GROUPS = 8
TOPK_GROUPS = 4
GROUP_SIZE = N_EXPERTS // N_GROUPS
ROUTED_SCALE = 2.5
ROPE_BASE = 10000.0
LN_EPS = 1e-5
GN_EPS = 1e-6
DN_ALPHA = (2 * DEPTH) ** 0.25

RET_QK_W = RET_HEADS * RET_DK
RET_V_W = RET_HEADS * RET_DV
ML_QK_W = ML_HEADS * ML_DK
ML_V_W = ML_HEADS * ML_DV

LANES = 128
SUBLANES = 8
D_MODEL = 2048
ROW_SUB = D_MODEL // LANES
MXU_N = 256
VMEM_LIMIT = 56 * 1024 * 1024

EXPERT_ROWS = 512
DISPATCH_ROWS = 512
IN_PROJ_TM = 1024
IN_PROJ_TN = 2 * MXU_N


def _cparams(sem):
    return pltpu.CompilerParams(dimension_semantics=sem, vmem_limit_bytes=VMEM_LIMIT)


def _dot(a, b):
    return jnp.dot(a, b, preferred_element_type=F32)


def _dot_nt(a, b):
    return lax.dot_general(a, b, (((1,), (1,)), ((), ())), preferred_element_type=F32)


def _dot_tn(a, b):
    return lax.dot_general(a, b, (((0,), (0,)), ((), ())), preferred_element_type=F32)


def _sigmoid(x):
    return 1.0 / (1.0 + jnp.exp(-x))


def _silu(x):
    return x * _sigmoid(x)


def _pack_rows(ref, val):
    n = val.shape[0]
    for s in range(ROW_SUB):
        ref[pl.ds(s, n, stride=ROW_SUB), :] = val[:, s * LANES:(s + 1) * LANES]


def _unpack_rows(ref, n):
    return jnp.concatenate([ref[pl.ds(s, n, stride=ROW_SUB), :] for s in range(ROW_SUB)], axis=1)


def _ada_kernel(c_ref, w_ref, b_ref, o_ref):
    s = _silu(c_ref[...]).astype(BF16)
    o_ref[...] = _dot(s, w_ref[...].astype(BF16)) + b_ref[...]


def _ada_mod(c, w_ada, b_ada):
    bsz, d = c.shape
    n = w_ada.shape[1]
    tn = 512
    return pl.pallas_call(
        _ada_kernel,
        grid=(n // tn,),
        in_specs=[pl.BlockSpec((bsz, d), lambda j: (0, 0)),
                  pl.BlockSpec((d, tn), lambda j: (0, j)),
                  pl.BlockSpec((1, tn), lambda j: (0, j))],
        out_specs=pl.BlockSpec((bsz, tn), lambda j: (0, j)),
        out_shape=jax.ShapeDtypeStruct((bsz, n), F32),
        compiler_params=_cparams(("arbitrary",)),
        name="ada_mod",
    )(c, w_ada, b_ada.reshape(1, n))


def _inproj_kernel(x_ref, sc_ref, sh_ref, w_ref, cos_ref, sin_ref, pb_ref, pf_ref, u_scr,
                   *, n_rot, n_b):
    j = pl.program_id(1)

    @pl.when(j == 0)
    def _():
        u_scr[...] = (x_ref[...] * (1.0 + sc_ref[...]) + sh_ref[...]).astype(BF16)

    acc = _dot(u_scr[...], w_ref[...])

    @pl.when(j < n_rot)
    def _():
        half = RET_DK // 2
        c = cos_ref[...]
        s = sin_ref[...]
        scale = jnp.where(j < n_rot // 2, 1.0, RET_DK ** -0.5).astype(F32)
        for h in range(acc.shape[1] // RET_DK):
            x1 = acc[:, h * RET_DK:h * RET_DK + half]
            x2 = acc[:, h * RET_DK + half:(h + 1) * RET_DK]
            o = jnp.concatenate([x1 * c - x2 * s, x1 * s + x2 * c], axis=1) * scale
            pb_ref[:, h * RET_DK:(h + 1) * RET_DK] = o.astype(BF16)

    @pl.when((j >= n_rot) & (j < n_b))
    def _():
        pb_ref[...] = acc.astype(BF16)

    @pl.when(j >= n_b)
    def _():
        pf_ref[...] = acc


def _in_proj(x2d, mod3, w_all, cos_t, sin_t, seq, tm):
    t, d = x2d.shape
    n_all = w_all.shape[1]
    tn = IN_PROJ_TN
    n_tiles = n_all // tn
    n_rot = 2 * RET_QK_W // tn
    n_b = (2 * RET_QK_W + RET_V_W + ML_V_W) // tn
    n_f = n_tiles - n_b
    tiles_per_seq = seq // tm
    kern = functools.partial(_inproj_kernel, n_rot=n_rot, n_b=n_b)
    return pl.pallas_call(
        kern,
        grid=(t // tm, n_tiles),
        in_specs=[
            pl.BlockSpec((tm, d), lambda i, j: (i, 0)),
            pl.BlockSpec((None, 1, d), lambda i, j: ((i // tiles_per_seq) * 6 + 1, 0, 0)),
            pl.BlockSpec((None, 1, d), lambda i, j: ((i // tiles_per_seq) * 6 + 0, 0, 0)),
            pl.BlockSpec((d, tn), lambda i, j: (0, j)),
            pl.BlockSpec((tm, RET_DK // 2), lambda i, j: (i % tiles_per_seq, 0)),
            pl.BlockSpec((tm, RET_DK // 2), lambda i, j: (i % tiles_per_seq, 0)),
        ],
        out_specs=[
            pl.BlockSpec((tm, tn), lambda i, j: (i, jnp.minimum(j, n_b - 1))),
            pl.BlockSpec((tm, tn), lambda i, j: (i, jnp.maximum(j - n_b, 0))),
        ],
        out_shape=[jax.ShapeDtypeStruct((t, n_b * tn), BF16),
                   jax.ShapeDtypeStruct((t, n_f * tn), F32)],
        scratch_shapes=[pltpu.VMEM((tm, d), BF16)],
        compiler_params=_cparams(("arbitrary", "arbitrary")),
        name="in_proj",
    )(x2d, mod3, mod3, w_all, cos_t, sin_t)


def _head_norm(h, w):
    mu = jnp.mean(h, axis=-1, keepdims=True)
    hc = h - mu
    var = jnp.mean(hc * hc, axis=-1, keepdims=True)
    return hc * lax.rsqrt(var + GN_EPS) * w


def _retention_kernel(q_ref, k_ref, v_ref, g_ref, nw_ref, o_ref, state_ref, *, n_chunks):
    L = CHUNK

    @pl.when(pl.program_id(1) == 0)
    def _():
        state_ref[...] = jnp.zeros_like(state_ref)

    ii = lax.broadcasted_iota(I32, (L, L), 0)
    jj = lax.broadcasted_iota(I32, (L, L), 1)
    diff = (ii - jj).astype(F32)
    causal = ii >= jj
    row = lax.broadcasted_iota(I32, (L, 1), 0).astype(F32)
    consts = []
    for h in range(RET_HEADS):
        log_g = math.log1p(-(2.0 ** (-5.0 - h)))
        intra = jnp.where(causal, jnp.exp(log_g * jnp.maximum(diff, 0.0)), 0.0)
        q_decay = jnp.exp(log_g * (row + 1.0))
        k_decay = jnp.exp(log_g * (L - 1.0 - row))
        consts.append((intra, q_decay, k_decay, math.exp(log_g * L)))

    def chunk_body(c, carry):
        rows = pl.ds(pl.multiple_of(c * L, L), L)
        for h in range(RET_HEADS):
            intra, q_decay, k_decay, chunk_decay = consts[h]
            qs = slice(h * RET_DK, (h + 1) * RET_DK)
            vs = slice(h * RET_DV, (h + 1) * RET_DV)
            qb = q_ref[rows, qs]
            kb = k_ref[rows, qs]
            vb = v_ref[rows, vs]
            st = state_ref[h]
            s = _dot_nt(qb, kb) * intra
            out = _dot(s.astype(BF16), vb) + q_decay * _dot(qb, st.astype(BF16))
            kd = (kb.astype(F32) * k_decay).astype(BF16)
            state_ref[h] = chunk_decay * st + _dot_tn(kd, vb)
            r = _head_norm(out, nw_ref[:, vs]) * _silu(g_ref[rows, vs])
            o_ref[rows, vs] = r.astype(BF16)
        return carry

    lax.fori_loop(0, n_chunks, chunk_body, 0)


def _retention(pb, pf, norm_w, bsz, seq, ts):
    t = pb.shape[0]
    n_s = seq // ts
    kern = functools.partial(_retention_kernel, n_chunks=ts // CHUNK)
    blk = lambda col: pl.BlockSpec((ts, RET_QK_W), lambda b, s: (b * n_s + s, col))
    return pl.pallas_call(
        kern,
        grid=(bsz, n_s),
        in_specs=[blk(0), blk(1), blk(2), blk(0),
                  pl.BlockSpec((1, RET_V_W), lambda b, s: (0, 0))],
        out_specs=pl.BlockSpec((ts, RET_V_W), lambda b, s: (b * n_s + s, 0)),
        out_shape=jax.ShapeDtypeStruct((t, RET_V_W), BF16),
        scratch_shapes=[pltpu.VMEM((RET_HEADS, RET_DK, RET_DV), F32)],
        compiler_params=_cparams(("arbitrary", "arbitrary")),
        name="retention",
    )(pb, pb, pb, pf, norm_w.reshape(1, RET_V_W))


def _split_hi_lo(x):
    hi = x.astype(BF16)
    lo = (x - hi.astype(F32)).astype(BF16)
    return hi, lo


def _mlstm_kernel(qk_ref, v_ref, o_ref, gt_ref, bg_ref, cw_ref, cb_ref, nw_ref, out_ref,
                  ext_ref, act_ref, c_ref, n_ref, m_ref, *, ts):
    L = CHUNK
    n_chunks = ts // L
    pad = SUBLANES

    @pl.when(pl.program_id(1) == 0)
    def _():
        ext_ref[0:pad, :] = jnp.zeros((pad, 2 * ML_QK_W), F32)
        c_ref[...] = jnp.zeros_like(c_ref)
        n_ref[...] = jnp.zeros_like(n_ref)
        m_ref[...] = jnp.zeros_like(m_ref)

    ext_ref[pad:pad + ts, :] = qk_ref[...]
    conv = cb_ref[...] + jnp.zeros((ts, 2 * ML_QK_W), F32)
    for w in range(CONV_W):
        off = pad - (CONV_W - 1) + w
        conv = conv + ext_ref[off:off + ts, :] * cw_ref[w:w + 1, :]
    act_ref[...] = _silu(conv)
    ext_ref[0:pad, :] = qk_ref[ts - pad:ts, :]

    g = gt_ref[...] + bg_ref[...]
    lf = jnp.minimum(g, 0.0) - jnp.log(1.0 + jnp.exp(-jnp.abs(g)))
    ti = lax.broadcasted_iota(I32, (ts, ts), 0)
    tj = lax.broadcasted_iota(I32, (ts, ts), 1)
    shift = L.bit_length() - 1
    tri = jnp.where((ti >= tj) & ((ti >> shift) == (tj >> shift)), 1.0, 0.0).astype(BF16)
    lf_hi, lf_lo = _split_hi_lo(lf)
    bcum = _dot(tri, lf_hi) + _dot(tri, lf_lo)
    rvec = g - pltpu.roll(bcum, LANES - ML_HEADS, 1)

    ii = lax.broadcasted_iota(I32, (L, L), 0)
    jj = lax.broadcasted_iota(I32, (L, L), 1)
    causal = ii >= jj
    lane = lax.broadcasted_iota(I32, (L, LANES), 1)

    for c in range(n_chunks):
        rows = slice(c * L, (c + 1) * L)
        r_hi, r_lo = _split_hi_lo(rvec[rows, :])
        for h in range(ML_HEADS):
            sel = jnp.where(lane == h, 1.0, 0.0).astype(BF16)
            rmat = _dot_nt(sel, r_hi) + _dot_nt(sel, r_lo)
            r_col = rvec[rows, h:h + 1]
            b_col = bcum[rows, ML_HEADS + h:ML_HEADS + h + 1]
            m_old = m_ref[h:h + 1, 0:1]
            cmax = jnp.max(jnp.where(causal, rmat, -jnp.inf), axis=1, keepdims=True)
            big_m = jnp.maximum(m_old, cmax)
            dmat = jnp.where(causal, jnp.exp(rmat - big_m), 0.0)
            w_inter = jnp.exp(m_old - big_m)
            floor = jnp.exp(-(b_col + big_m))

            qs = slice(h * ML_DK, (h + 1) * ML_DK)
            ks = slice(ML_QK_W + h * ML_DK, ML_QK_W + (h + 1) * ML_DK)
            vs = slice(h * ML_DV, (h + 1) * ML_DV)
            qf = act_ref[rows, qs] * (ML_DK ** -0.5)
            kf = act_ref[rows, ks]
            qb = qf.astype(BF16)
            vb = v_ref[rows, vs]
            cst = c_ref[h]
            nst = n_ref[h:h + 1, :]

            s = _dot_nt(qb, kf.astype(BF16)) * dmat
            num = _dot(s.astype(BF16), vb) + w_inter * _dot(qb, cst.astype(BF16))
            qn = jnp.sum(qb.astype(F32) * nst, axis=1, keepdims=True)
            den = jnp.sum(s, axis=1, keepdims=True) + w_inter * qn
            hout = num * (1.0 / jnp.maximum(jnp.abs(den), floor))

            m_last = big_m[L - 1:L, :]
            m_new = b_col[L - 1:L, :] + m_last
            g_state = jnp.exp(m_old - m_last)
            g_k = jnp.exp(r_col - m_last)
            kw = kf * g_k
            c_ref[h] = g_state * cst + _dot_tn(kw.astype(BF16), vb)
            n_ref[h:h + 1, :] = g_state * nst + jnp.sum(kw, axis=0, keepdims=True)
            m_ref[h:h + 1, :] = jnp.broadcast_to(m_new, (1, LANES))

            y = _head_norm(hout, nw_ref[:, vs]) * _sigmoid(o_ref[rows, vs])
            out_ref[rows, vs] = y.astype(BF16)


def _mlstm(pb, pf, b_gates, conv_w, conv_b, norm_w, bsz, seq, ts):
    t = pb.shape[0]
    n_s = seq // ts
    bg = jnp.zeros((1, LANES), F32).at[0, :2 * ML_HEADS].set(b_gates)
    kern = functools.partial(_mlstm_kernel, ts=ts)
    wide = 2 * ML_QK_W
    gate_blk = (RET_V_W + wide + ML_V_W) // LANES
    return pl.pallas_call(
        kern,
        grid=(bsz, n_s),
        in_specs=[
            pl.BlockSpec((ts, wide), lambda b, s: (b * n_s + s, RET_V_W // wide)),
            pl.BlockSpec((ts, ML_V_W), lambda b, s: (b * n_s + s, 3)),
            pl.BlockSpec((ts, ML_V_W), lambda b, s: (b * n_s + s, (RET_V_W + wide) // ML_V_W)),
            pl.BlockSpec((ts, LANES), lambda b, s: (b * n_s + s, gate_blk)),
            pl.BlockSpec((1, LANES), lambda b, s: (0, 0)),
            pl.BlockSpec((CONV_W, wide), lambda b, s: (0, 0)),
            pl.BlockSpec((1, wide), lambda b, s: (0, 0)),
            pl.BlockSpec((1, ML_V_W), lambda b, s: (0, 0)),
        ],
        out_specs=pl.BlockSpec((ts, ML_V_W), lambda b, s: (b * n_s + s, 0)),
        out_shape=jax.ShapeDtypeStruct((t, ML_V_W), BF16),
        scratch_shapes=[
            pltpu.VMEM((ts + SUBLANES, wide), F32),
            pltpu.VMEM((ts, wide), F32),
            pltpu.VMEM((ML_HEADS, ML_DK, ML_DV), F32),
            pltpu.VMEM((SUBLANES, ML_DK), F32),
            pltpu.VMEM((SUBLANES, LANES), F32),
        ],
        compiler_params=_cparams(("arbitrary", "arbitrary")),
        name="mlstm",
    )(pf, pb, pf, pf, bg, conv_w, conv_b.reshape(1, wide), norm_w.reshape(1, ML_V_W))


def _layer_norm(z, g, b):
    mu = jnp.mean(z, axis=-1, keepdims=True)
    zc = z - mu
    var = jnp.mean(zc * zc, axis=-1, keepdims=True)
    return zc * lax.rsqrt(var + LN_EPS) * g + b


def _outproj_kernel(r_ref, h_ref, w1_ref, w2_ref, x_ref, g1_ref, sc_ref, sh_ref, lng_ref, lnb_ref,
                    x1_ref, u2_ref):
    mix = _dot(r_ref[...], w1_ref[...]) + _dot(h_ref[...], w2_ref[...])
    z = DN_ALPHA * x_ref[...] + g1_ref[...] * mix
    x1 = _layer_norm(z, lng_ref[...], lnb_ref[...])
    x1_ref[...] = x1
    _pack_rows(u2_ref, x1 * (1.0 + sc_ref[...]) + sh_ref[...])


def _out_proj(r_mix, h_mix, w_out_bf, x2d, mod3, ln_g, ln_b, seq, tm):
    t, d = x2d.shape
    tiles_per_seq = seq // tm
    modblk = lambda k: pl.BlockSpec((None, 1, d), lambda i: ((i // tiles_per_seq) * 6 + k, 0, 0))
    return pl.pallas_call(
        _outproj_kernel,
        grid=(t // tm,),
        in_specs=[
            pl.BlockSpec((tm, RET_V_W), lambda i: (i, 0)),
            pl.BlockSpec((tm, ML_V_W), lambda i: (i, 0)),
            pl.BlockSpec((RET_V_W, d), lambda i: (0, 0)),
            pl.BlockSpec((ML_V_W, d), lambda i: (1, 0)),
            pl.BlockSpec((tm, d), lambda i: (i, 0)),
            modblk(2), modblk(4), modblk(3),
            pl.BlockSpec((1, d), lambda i: (0, 0)),
            pl.BlockSpec((1, d), lambda i: (0, 0)),
        ],
        out_specs=[pl.BlockSpec((tm, d), lambda i: (i, 0)),
                   pl.BlockSpec((tm * ROW_SUB, LANES), lambda i: (i, 0))],
        out_shape=[jax.ShapeDtypeStruct((t, d), F32),
                   jax.ShapeDtypeStruct((t * ROW_SUB, LANES), F32)],
        compiler_params=_cparams(("arbitrary",)),
        name="out_proj",
    )(r_mix, h_mix, w_out_bf, w_out_bf, x2d, mod3, mod3, mod3,
      ln_g.reshape(1, d), ln_b.reshape(1, d))


def _router_kernel(u_ref, w_ref, eb_ref, eidx_ref, wts_ref, rank_ref, cnt_ref, carry_ref, *, tm):
    @pl.when(pl.program_id(0) == 0)
    def _():
        carry_ref[...] = jnp.zeros_like(carry_ref)

    x = _unpack_rows(u_ref, tm).astype(BF16)
    logits = _dot_nt(w_ref[...], x)
    scores = _sigmoid(logits)
    biased = scores + eb_ref[:, 0:1]
    neg = -jnp.inf
    gi = lax.broadcasted_iota(I32, (N_GROUPS, tm), 0)
    sc = [scores[j * N_GROUPS:(j + 1) * N_GROUPS, :] for j in range(GROUP_SIZE)]
    bs = [biased[j * N_GROUPS:(j + 1) * N_GROUPS, :] for j in range(GROUP_SIZE)]
    eid = [gi * GROUP_SIZE + j for j in range(GROUP_SIZE)]

    m1 = functools.reduce(jnp.maximum, bs)
    first = functools.reduce(jnp.minimum,
                             [jnp.where(bs[j] == m1, j, GROUP_SIZE) for j in range(GROUP_SIZE)])
    m2 = functools.reduce(jnp.maximum,
                          [jnp.where(first == j, neg, bs[j]) for j in range(GROUP_SIZE)])
    grp = m1 + m2

    keep = jnp.zeros((N_GROUPS, tm), dtype=jnp.bool_)
    for _ in range(TOPK_GROUPS):
        top = jnp.max(grp, axis=0, keepdims=True)
        pick = jnp.min(jnp.where(grp == top, gi, N_GROUPS), axis=0, keepdims=True)
        hit = gi == pick
        keep = keep | hit
        grp = jnp.where(hit, neg, grp)

    cand = [jnp.where(keep, bs[j], neg) for j in range(GROUP_SIZE)]
    chosen = [jnp.zeros((N_GROUPS, tm), dtype=jnp.bool_) for _ in range(GROUP_SIZE)]
    picks, vals = [], []
    for _ in range(TOP_K):
        top = jnp.max(functools.reduce(jnp.maximum, cand), axis=0, keepdims=True)
        e_pick = jnp.min(functools.reduce(
            jnp.minimum, [jnp.where(cand[j] == top, eid[j], N_EXPERTS) for j in range(GROUP_SIZE)]),
            axis=0, keepdims=True)
        hits = [eid[j] == e_pick for j in range(GROUP_SIZE)]
        val = jnp.sum(functools.reduce(
            jnp.add, [jnp.where(hits[j], sc[j], 0.0) for j in range(GROUP_SIZE)]),
            axis=0, keepdims=True)
        cand = [jnp.where(hits[j], neg, cand[j]) for j in range(GROUP_SIZE)]
        chosen = [chosen[j] | hits[j] for j in range(GROUP_SIZE)]
        picks.append(e_pick)
        vals.append(val)

    total = functools.reduce(jnp.add, vals)
    inv = ROUTED_SCALE / total
    eidx_ref[...] = jnp.concatenate(picks, axis=0)
    wts_ref[...] = jnp.concatenate([v * inv for v in vals], axis=0)

    onehot = jnp.concatenate([jnp.where(chosen[j], 1.0, 0.0) for j in range(GROUP_SIZE)], axis=0)
    ti = lax.broadcasted_iota(I32, (tm, tm), 0)
    tj = lax.broadcasted_iota(I32, (tm, tm), 1)
    before = jnp.where(ti < tj, 1.0, 0.0).astype(BF16)
    carry = carry_ref[...]
    prior = _dot(onehot.astype(BF16), before) + carry[:, 0:1]
    pr = [prior[j * N_GROUPS:(j + 1) * N_GROUPS, :] for j in range(GROUP_SIZE)]
    ranks = []
    for k in range(TOP_K):
        ranks.append(jnp.sum(functools.reduce(
            jnp.add, [jnp.where(eid[j] == picks[k], pr[j], 0.0) for j in range(GROUP_SIZE)]),
            axis=0, keepdims=True))
    rank_ref[...] = jnp.concatenate(ranks, axis=0).astype(I32)
    carry = carry + jnp.sum(onehot, axis=1, keepdims=True)
    carry_ref[...] = carry
    cnt_ref[...] = carry.astype(I32)


def _router(u2p, w_router, e_bias, t, tm):
    d = w_router.shape[0]
    perm = np.arange(N_EXPERTS).reshape(N_GROUPS, GROUP_SIZE).T.reshape(-1)
    w_t = w_router.T[perm].astype(BF16)
    eb = jnp.broadcast_to(e_bias[perm][:, None], (N_EXPERTS, LANES)).astype(F32)
    kern = functools.partial(_router_kernel, tm=tm)
    eidx, wts, rank, cnt = pl.pallas_call(
        kern,
        grid=(t // tm,),
        in_specs=[pl.BlockSpec((tm * ROW_SUB, LANES), lambda i: (i, 0)),
                  pl.BlockSpec((N_EXPERTS, d), lambda i: (0, 0)),
                  pl.BlockSpec((N_EXPERTS, LANES), lambda i: (0, 0))],
        out_specs=[pl.BlockSpec((TOP_K, tm), lambda i: (0, i)),
                   pl.BlockSpec((TOP_K, tm), lambda i: (0, i)),
                   pl.BlockSpec((TOP_K, tm), lambda i: (0, i)),
                   pl.BlockSpec((N_EXPERTS, LANES), lambda i: (0, 0))],
        out_shape=[jax.ShapeDtypeStruct((TOP_K, t), I32),
                   jax.ShapeDtypeStruct((TOP_K, t), F32),
                   jax.ShapeDtypeStruct((TOP_K, t), I32),
                   jax.ShapeDtypeStruct((N_EXPERTS, LANES), I32)],
        scratch_shapes=[pltpu.VMEM((N_EXPERTS, LANES), F32)],
        compiler_params=_cparams(("arbitrary",)),
        name="router",
    )(u2p, w_t, eb)
    counts = jnp.zeros((N_EXPERTS,), I32).at[perm].set(cnt[:, 0])
    return eidx, wts, rank, counts


def _dispatch_kernel(cnt_ref, start_ref, dest_ref, u_ref, xs_ref, zero_ref, sem, pad_sem,
                     *, tm, block_rows):
    def row_dst(row):
        return xs_ref.at[pl.ds(pl.multiple_of(row * ROW_SUB, ROW_SUB), ROW_SUB)]

    @pl.when(pl.program_id(0) == 0)
    def _():
        zero_ref[...] = jnp.zeros_like(zero_ref)

        def per_expert(e, carry):
            lo = start_ref[e] + cnt_ref[e]
            hi = start_ref[e] + (cnt_ref[e] + block_rows - 1) // block_rows * block_rows

            def issue(r, c):
                pltpu.make_async_copy(zero_ref, row_dst(r), pad_sem).start()
                return c

            def drain(r, c):
                pltpu.make_async_copy(zero_ref, row_dst(r), pad_sem).wait()
                return c

            lax.fori_loop(lo, hi, issue, 0)
            lax.fori_loop(lo, hi, drain, 0)
            return carry

        lax.fori_loop(0, N_EXPERTS, per_expert, 0)

    def per_token(t, carry):
        src = u_ref.at[pl.ds(pl.multiple_of(t * ROW_SUB, ROW_SUB), ROW_SUB)]
        for k in range(TOP_K):
            pltpu.make_async_copy(src, row_dst(dest_ref[0, 0, t * TOP_K + k]), sem).start()
        return carry

    lax.fori_loop(0, tm, per_token, 0)
    for _ in range(TOP_K):
        pltpu.make_async_copy(u_ref, xs_ref.at[pl.ds(0, tm * ROW_SUB)], sem).wait()


def _dispatch(u2p, dest_t, counts, pad_start, n_rows, t, tm, block_rows):
    kern = functools.partial(_dispatch_kernel, tm=tm, block_rows=block_rows)
    grid_spec = pltpu.PrefetchScalarGridSpec(
        num_scalar_prefetch=2,
        grid=(t // tm,),
        in_specs=[pl.BlockSpec((1, 1, tm * TOP_K), lambda i, c, s: (i, 0, 0),
                               memory_space=pltpu.SMEM),
                  pl.BlockSpec((tm * ROW_SUB, LANES), lambda i, c, s: (i, 0))],
        out_specs=pl.BlockSpec(memory_space=pl.ANY),
        scratch_shapes=[pltpu.VMEM((ROW_SUB, LANES), F32),
                        pltpu.SemaphoreType.DMA, pltpu.SemaphoreType.DMA],
    )
    return pl.pallas_call(
        kern,
        grid_spec=grid_spec,
        out_shape=jax.ShapeDtypeStruct((n_rows * ROW_SUB, LANES), F32),
        compiler_params=_cparams(("arbitrary",)),
        name="dispatch",
    )(counts, pad_start, dest_t.reshape(t // tm, 1, tm * TOP_K), u2p)


def _expert_kernel(be_ref, nu_ref, xs_ref, wg_ref, wu_ref, wd_ref, ys_ref, *, rows):
    @pl.when(pl.program_id(0) < nu_ref[0])
    def _():
        x = _unpack_rows(xs_ref, rows).astype(BF16)
        gate = _dot(x, wg_ref[...])
        up = _dot(x, wu_ref[...])
        hidden = (_silu(gate) * up).astype(BF16)
        _pack_rows(ys_ref, _dot(hidden, wd_ref[...]))


def _experts(xs, block_e, n_used, wg, wu, wd, rows):
    n_blocks = block_e.shape[0]
    _, d, f = wg.shape
    kern = functools.partial(_expert_kernel, rows=rows)
    row_blk = pl.BlockSpec((rows * ROW_SUB, LANES),
                           lambda i, be, nu: (jnp.minimum(i, nu[0] - 1), 0))
    grid_spec = pltpu.PrefetchScalarGridSpec(
        num_scalar_prefetch=2,
        grid=(n_blocks,),
        in_specs=[row_blk,
                  pl.BlockSpec((None, d, f), lambda i, be, nu: (be[i], 0, 0)),
                  pl.BlockSpec((None, d, f), lambda i, be, nu: (be[i], 0, 0)),
                  pl.BlockSpec((None, f, d), lambda i, be, nu: (be[i], 0, 0))],
        out_specs=row_blk,
    )
    return pl.pallas_call(
        kern,
        grid_spec=grid_spec,
        out_shape=jax.ShapeDtypeStruct(xs.shape, F32),
        compiler_params=_cparams(("arbitrary",)),
        name="experts",
    )(block_e, n_used, xs, wg, wu, wd)


def _combine_kernel(dcur_ref, dnext_ref, ys_ref, w_ref, u_ref, x1_ref, g2_ref, wsg_ref, wsu_ref,
                    wsd_ref, lng_ref, lnb_ref, o_ref, ybuf, y_scr, sem, *, tm, n_tiles):
    i = pl.program_id(0)
    slot = i % 2

    def issue(dref, into):
        def per_token(t, carry):
            for k in range(TOP_K):
                r = t * TOP_K + k
                pltpu.make_async_copy(
                    ys_ref.at[pl.ds(pl.multiple_of(dref[0, 0, r] * ROW_SUB, ROW_SUB), ROW_SUB)],
                    ybuf.at[into, pl.ds(pl.multiple_of(r * ROW_SUB, ROW_SUB), ROW_SUB)],
                    sem.at[into]).start()
            return carry

        lax.fori_loop(0, tm, per_token, 0)

    @pl.when(i == 0)
    def _():
        issue(dcur_ref, 0)

    @pl.when(i + 1 < n_tiles)
    def _():
        issue(dnext_ref, 1 - slot)

    pltpu.make_async_copy(ys_ref.at[pl.ds(0, tm * TOP_K * ROW_SUB)], ybuf.at[slot],
                          sem.at[slot]).wait()
    yg_ref = ybuf.at[slot]

    wk = [jnp.broadcast_to(w_ref[:, k:k + 1], (tm, LANES)) for k in range(TOP_K)]
    for s in range(ROW_SUB):
        acc = jnp.zeros((tm, LANES), F32)
        for k in range(TOP_K):
            acc = acc + wk[k] * yg_ref[pl.ds(k * ROW_SUB + s, tm, stride=TOP_K * ROW_SUB), :]
        y_scr[:, s * LANES:(s + 1) * LANES] = acc
    x = _unpack_rows(u_ref, tm).astype(BF16)
    hidden = (_silu(_dot(x, wsg_ref[...])) * _dot(x, wsu_ref[...])).astype(BF16)
    y = y_scr[...] + _dot(hidden, wsd_ref[...])
    z = DN_ALPHA * x1_ref[...] + g2_ref[...] * y
    o_ref[...] = _layer_norm(z, lng_ref[...], lnb_ref[...])


def _combine(ys, dest_t, wts_t, u2p, x1, mod3, wsg, wsu, wsd, ln_g, ln_b, seq, tm):
    t, d = x1.shape
    f = wsg.shape[1]
    tiles_per_seq = seq // tm
    n_tiles = t // tm
    kern = functools.partial(_combine_kernel, tm=tm, n_tiles=n_tiles)
    dest3 = dest_t.reshape(n_tiles, 1, tm * TOP_K)
    return pl.pallas_call(
        kern,
        grid=(n_tiles,),
        in_specs=[
            pl.BlockSpec((1, 1, tm * TOP_K), lambda i: (i, 0, 0), memory_space=pltpu.SMEM),
            pl.BlockSpec((1, 1, tm * TOP_K), lambda i: (jnp.minimum(i + 1, n_tiles - 1), 0, 0),
                         memory_space=pltpu.SMEM),
            pl.BlockSpec(memory_space=pl.ANY),
            pl.BlockSpec((tm, TOP_K), lambda i: (i, 0)),
            pl.BlockSpec((tm * ROW_SUB, LANES), lambda i: (i, 0)),
            pl.BlockSpec((tm, d), lambda i: (i, 0)),
            pl.BlockSpec((None, 1, d), lambda i: ((i // tiles_per_seq) * 6 + 5, 0, 0)),
            pl.BlockSpec((d, f), lambda i: (0, 0)),
            pl.BlockSpec((d, f), lambda i: (0, 0)),
            pl.BlockSpec((f, d), lambda i: (0, 0)),
            pl.BlockSpec((1, d), lambda i: (0, 0)),
            pl.BlockSpec((1, d), lambda i: (0, 0)),
        ],
        out_specs=pl.BlockSpec((tm, d), lambda i: (i, 0)),
        out_shape=jax.ShapeDtypeStruct((t, d), F32),
        scratch_shapes=[pltpu.VMEM((2, tm * TOP_K * ROW_SUB, LANES), F32),
                        pltpu.VMEM((tm, d), F32),
                        pltpu.SemaphoreType.DMA((2,))],
        compiler_params=_cparams(("arbitrary",)),
        name="combine",
    )(dest3, dest3, ys, wts_t, u2p, x1, mod3, wsg, wsu, wsd, ln_g.reshape(1, d), ln_b.reshape(1, d))


def _pick_tile(n, pref):
    while n % pref:
        pref //= 2
    return pref


def _prep_w_in(w_in):
    d = w_in.shape[0]
    o = 0
    rq = w_in[:, o:o + RET_QK_W]; o += RET_QK_W
    rk = w_in[:, o:o + RET_QK_W]; o += RET_QK_W
    rv = w_in[:, o:o + RET_V_W]; o += RET_V_W
    rg = w_in[:, o:o + RET_V_W]; o += RET_V_W
    mqk = w_in[:, o:o + 2 * ML_QK_W]; o += 2 * ML_QK_W
    mv = w_in[:, o:o + ML_V_W]; o += ML_V_W
    mo = w_in[:, o:o + ML_V_W]; o += ML_V_W
    gates = w_in[:, o:o + 2 * ML_HEADS]

    def split_pairs(w):
        return w.reshape(d, RET_HEADS, RET_DK // 2, 2).transpose(0, 1, 3, 2).reshape(d, RET_QK_W)

    gates_pad = jnp.pad(gates, ((0, 0), (0, IN_PROJ_TN - 2 * ML_HEADS)))
    return jnp.concatenate([split_pairs(rq), split_pairs(rk), rv, mv, rg, mqk, mo, gates_pad],
                           axis=1).astype(BF16)


def _layer(x, c, w_ada, b_ada, w_in, b_gates, conv_w, conv_b, ret_norm_w, ml_norm_w, w_out,
           ln1_g, ln1_b, w_router, e_bias, w_gate, w_up, w_down, ws_gate, ws_up, ws_down,
           ln2_g, ln2_b, expert_rows, dispatch_rows):
    bsz, seq, d = x.shape
    t = bsz * seq
    x2d = x.reshape(t, d)

    mod = _ada_mod(c, w_ada, b_ada)
    mod3 = mod.reshape(bsz * 6, 1, d)

    inv = ROPE_BASE ** (-jnp.arange(0, RET_DK, 2, dtype=F32) / RET_DK)
    ang = jnp.arange(seq, dtype=F32)[:, None] * inv
    pb, pf = _in_proj(x2d, mod3, _prep_w_in(w_in), jnp.cos(ang), jnp.sin(ang), seq,
                      _pick_tile(seq, IN_PROJ_TM))

    ts = _pick_tile(seq, 256)
    r_mix = _retention(pb, pf, ret_norm_w, bsz, seq, ts)
    h_mix = _mlstm(pb, pf, b_gates, conv_w, conv_b, ml_norm_w, bsz, seq, ts)

    x1, u2p = _out_proj(r_mix, h_mix, w_out.astype(BF16), x2d, mod3, ln1_g, ln1_b, seq,
                        _pick_tile(seq, 256))

    eidx, wts, rank, counts = _router(u2p, w_router, e_bias, t, _pick_tile(t, 512))

    g = expert_rows
    a = t * TOP_K
    n_blocks = -(-(a + N_EXPERTS * (g - 1)) // g)
    padded = (counts + g - 1) // g * g
    pad_end = jnp.cumsum(padded).astype(I32)
    pad_start = pad_end - padded
    experts = jnp.arange(N_EXPERTS, dtype=I32)[:, None, None]
    dest = jnp.sum(jnp.where(eidx[None] == experts, pad_start[:, None, None], 0), axis=0) + rank
    dest_t = dest.T.reshape(-1).astype(I32)
    n_used = pad_end[-1] // g
    first_row = jnp.minimum(jnp.arange(n_blocks, dtype=I32), n_used - 1) * g
    block_e = jnp.minimum(jnp.sum((pad_end[None, :] <= first_row[:, None]).astype(I32), axis=1),
                          N_EXPERTS - 1)

    xs = _dispatch(u2p, dest_t, counts, pad_start, n_blocks * g, t, _pick_tile(t, dispatch_rows), g)
    ys = _experts(xs, block_e, n_used.reshape(1), w_gate.astype(BF16), w_up.astype(BF16),
                  w_down.astype(BF16), g)
    out = _combine(ys, dest_t, wts.T, u2p, x1, mod3, ws_gate.astype(BF16), ws_up.astype(BF16),
                   ws_down.astype(BF16), ln2_g, ln2_b, seq, _pick_tile(seq, 128))
    return out.reshape(bsz, seq, d)


def kernel(x, c, w_ada, b_ada, w_in, b_gates, conv_w, conv_b, ret_norm_w, ml_norm_w, w_out, ln1_g,
           ln1_b, w_router, e_bias, w_gate, w_up, w_down, ws_gate, ws_up, ws_down, ln2_g, ln2_b):
    for l in range(DEPTH):
        x = _layer(x, c, w_ada[l], b_ada[l], w_in[l], b_gates[l], conv_w[l], conv_b[l],
                   ret_norm_w[l], ml_norm_w[l], w_out[l], ln1_g[l], ln1_b[l], w_router[l],
                   e_bias[l], w_gate[l], w_up[l], w_down[l], ws_gate[l], ws_up[l], ws_down[l],
                   ln2_g[l], ln2_b[l], EXPERT_ROWS, DISPATCH_ROWS)
    return x
```
